```python
import math
import jax
import jax.numpy as jnp
from jax import lax
import numpy as np

D_MODEL = 1024
BATCH = 1
SEQ = 16384
DEPTH = 2
DEC_BATCH = 128
DEC_SEQ = 4
PAST_LEN = 16384
PAGE_SIZE = 128

BRANCH_W = 512
N_BRANCH = 4
NSA_H = 8
NSA_KV = 1
NSA_DH = 64
NSA_BLOCK = 64
NSA_TOPN = 16
NSA_WINDOW = 512
MLA_H = 8
MLA_Q_RANK = 256
MLA_KV_RANK = 128
MLA_NOPE = 64
MLA_ROPE = 32
MLA_DV = 64
ROPE_THETA = 10000.0
DIFF_H = 4
DIFF_KV = 1
DIFF_DH = 64
N_MEM = 256
MEM_H = 4
MEM_DH = 128
N_BUCKETS = 32
MAX_DISTANCE = 128
BIAS_H = NSA_H + DIFF_H
Q_BLOCK = 128
EPS = 1e-6
D_IN = (NSA_H * NSA_DH + 6 * NSA_KV * NSA_DH + 3 * NSA_H + MLA_Q_RANK + MLA_KV_RANK + MLA_ROPE + DIFF_H * 2 * DIFF_DH + DIFF_KV * 4 * DIFF_DH + MEM_H * MEM_DH + N_BRANCH * BRANCH_W + N_BRANCH * D_MODEL)

kernel_name = 'hybrid_nsa_mla_diff_memory_decoder_step'


def _rms(x, g):
    xf = x.astype(jnp.float32)
    y = xf * lax.rsqrt(jnp.mean(xf * xf, axis=-1, keepdims=True) + EPS)
    return (y * g.astype(jnp.float32)).astype(x.dtype)


def _masked_softmax(s, mask):
    s = jnp.where(mask, s.astype(jnp.float32), -jnp.inf)
    m = jnp.max(s, axis=-1, keepdims=True)
    m = jnp.where(jnp.isfinite(m), m, 0.0)
    e = jnp.where(mask, jnp.exp(s - m), 0.0)
    return e / jnp.maximum(jnp.sum(e, axis=-1, keepdims=True), jnp.finfo(jnp.float32).tiny)


def _rel_bucket(dist):
    n = jnp.maximum(dist, 0)
    exact = N_BUCKETS // 2
    nf = jnp.maximum(n, exact).astype(jnp.float32)
    large = exact + (jnp.log(nf / exact) / math.log(MAX_DISTANCE / exact) * (N_BUCKETS - exact)).astype(jnp.int32)
    return jnp.where(n < exact, n, jnp.minimum(large, N_BUCKETS - 1)).astype(jnp.int32)


def _rope(x, pos):
    half = x.shape[-1] // 2
    inv = ROPE_THETA ** (-jnp.arange(half, dtype=jnp.float32) / half)
    ang = pos.astype(jnp.float32)[:, None] * inv[None, :]
    shape = (1, pos.shape[0]) + (1,) * (x.ndim - 3) + (half,)
    cos = jnp.cos(ang).reshape(shape)
    sin = jnp.sin(ang).reshape(shape)
    xf = x.astype(jnp.float32)
    x1, x2 = xf[..., :half], xf[..., half:]
    return jnp.concatenate([x1 * cos - x2 * sin, x1 * sin + x2 * cos], axis=-1).astype(x.dtype)


def _split_in(u):
    widths = [NSA_H * NSA_DH, 2 * NSA_KV * NSA_DH, 2 * NSA_KV * NSA_DH, 2 * NSA_KV * NSA_DH, 3 * NSA_H,
              MLA_Q_RANK, MLA_KV_RANK, MLA_ROPE,
              DIFF_H * 2 * DIFF_DH, DIFF_KV * 4 * DIFF_DH, MEM_H * MEM_DH,
              N_BRANCH * BRANCH_W, N_BRANCH * D_MODEL]
    idx = np.cumsum(widths)[:-1].tolist()
    return jnp.split(u, idx, axis=-1)


def _features(h, pos, lw):
    B, T = h.shape[0], h.shape[1]
    u = h @ lw['w_in']
    (nsa_q, nsa_cmp, nsa_slc, nsa_win, nsa_gate, mla_cq, mla_ckv, mla_kr,
     diff_q, diff_kv, mem_q, z, mg) = _split_in(u)
    cq = _rms(mla_cq, lw['mla_q_norm_g'])
    q = (cq @ lw['w_mla_uq']).reshape(B, T, MLA_H, MLA_NOPE + MLA_ROPE)
    w_uk = lw['w_mla_uk'].reshape(MLA_KV_RANK, MLA_H, MLA_NOPE)
    f = {
        'nsa_q': nsa_q.reshape(B, T, NSA_H, NSA_DH),
        'nsa_cmp': nsa_cmp.reshape(B, T, NSA_KV, 2 * NSA_DH),
        'nsa_slc': nsa_slc.reshape(B, T, NSA_KV, 2 * NSA_DH),
        'nsa_win': nsa_win.reshape(B, T, NSA_KV, 2 * NSA_DH),
        'nsa_gate': jax.nn.sigmoid(nsa_gate.reshape(B, T, NSA_H, 3)),
        'mla_q_lat': jnp.einsum('bthn,rhn->bthr', q[..., :MLA_NOPE], w_uk),
        'mla_q_rope': _rope(q[..., MLA_NOPE:], pos),
        'mla_latent': jnp.concatenate([_rms(mla_ckv, lw['mla_kv_norm_g']), _rope(mla_kr, pos)], axis=-1),
        'diff_q': diff_q.reshape(B, T, DIFF_H, 2 * DIFF_DH),
        'diff_kv': diff_kv.reshape(B, T, DIFF_KV, 4 * DIFF_DH),
        'mem_q': mem_q.reshape(B, T, MEM_H, MEM_DH),
        'z': z.reshape(B, T, N_BRANCH, BRANCH_W),
        'mg': mg.reshape(B, T, N_BRANCH, D_MODEL),
    }
    return f


def _nsa_attend(q, q_pos, gate, kvc, fetch_slc, kvw, w_pos, tbl):
    B, Tq = q.shape[0], q.shape[1]
    G = NSA_H // NSA_KV
    qg = q.reshape(B, Tq, NSA_KV, G, NSA_DH) * (NSA_DH ** -0.5)
    tblg = tbl.reshape(N_BUCKETS, NSA_KV, G)
    kv_ix = jnp.arange(NSA_KV)[None, None, :, None]
    nb = kvc.shape[1]
    blk_end = (jnp.arange(nb) + 1) * NSA_BLOCK - 1
    valid_c = blk_end[None, :] <= q_pos[:, None]
    bias_c = jnp.transpose(tblg[_rel_bucket(q_pos[:, None] - blk_end[None, :])], (0, 2, 3, 1))
    s_c = jnp.einsum('bqkgd,bnkd->bqkgn', qg, kvc[..., :NSA_DH]) + bias_c
    p_c = _masked_softmax(s_c, valid_c[None, :, None, None, :])
    o_c = jnp.einsum('bqkgn,bnkd->bqkgd', p_c.astype(kvc.dtype), kvc[..., NSA_DH:])
    imp = jnp.where(valid_c[None, :, None, :], jnp.sum(p_c, axis=3), -1.0)
    top_v, top_i = lax.top_k(imp, min(NSA_TOPN, nb))
    cur = jnp.broadcast_to((q_pos // NSA_BLOCK)[None, :, None, None], (B, Tq, NSA_KV, 1)).astype(top_i.dtype)
    blk = jnp.concatenate([top_i, cur], axis=-1)
    blk_ok = jnp.concatenate([top_v >= 0.0, jnp.ones(cur.shape, bool)], axis=-1)
    n_sel = blk.shape[-1]
    kpos5 = blk[..., None] * NSA_BLOCK + jnp.arange(NSA_BLOCK)
    kvs = fetch_slc(kpos5).reshape(B, Tq, NSA_KV, n_sel * NSA_BLOCK, 2 * NSA_DH)
    kpos = kpos5.reshape(B, Tq, NSA_KV, n_sel * NSA_BLOCK)
    dist_s = q_pos[None, :, None, None] - kpos
    ok_s = jnp.repeat(blk_ok, NSA_BLOCK, axis=-1) & (dist_s >= 0)
    bias_s = jnp.moveaxis(tblg[_rel_bucket(dist_s), kv_ix], -1, 3)
    s_s = jnp.einsum('bqkgd,bqkmd->bqkgm', qg, kvs[..., :NSA_DH]) + bias_s
    p_s = _masked_softmax(s_s, ok_s[:, :, :, None, :])
    o_s = jnp.einsum('bqkgm,bqkmd->bqkgd', p_s.astype(kvs.dtype), kvs[..., NSA_DH:])
    dist_w = q_pos[:, None] - w_pos[None, :]
    ok_w = (dist_w >= 0) & (dist_w <= NSA_WINDOW) & (w_pos[None, :] >= 0)
    bias_w = jnp.transpose(tblg[_rel_bucket(dist_w)], (0, 2, 3, 1))
    s_w = jnp.einsum('bqkgd,btkd->bqkgt', qg, kvw[..., :NSA_DH]) + bias_w
    p_w = _masked_softmax(s_w, ok_w[None, :, None, None, :])
    o_w = jnp.einsum('bqkgt,btkd->bqkgd', p_w.astype(kvw.dtype), kvw[..., NSA_DH:])
    g = gate.reshape(B, Tq, NSA_KV, G, 3).astype(o_c.dtype)
    o = g[..., 0:1] * o_c + g[..., 1:2] * o_s + g[..., 2:3] * o_w
    return o.reshape(B, Tq, NSA_H * NSA_DH)


def _mla_attend(q_lat, q_rope, lat, q_pos, k_pos, w_uv):
    B, Tq = q_lat.shape[0], q_lat.shape[1]
    ckv, kr = lat[..., :MLA_KV_RANK], lat[..., MLA_KV_RANK:]
    s = (jnp.einsum('bqhr,btr->bqht', q_lat, ckv) + jnp.einsum('bqhe,bte->bqht', q_rope, kr)) * ((MLA_NOPE + MLA_ROPE) ** -0.5)
    p = _masked_softmax(s, (k_pos[None, :] <= q_pos[:, None])[None, :, None, :])
    o_lat = jnp.einsum('bqht,btr->bqhr', p.astype(lat.dtype), ckv)
    o = jnp.einsum('bqhr,rhv->bqhv', o_lat, w_uv.reshape(MLA_KV_RANK, MLA_H, MLA_DV))
    return o.reshape(B, Tq, MLA_H * MLA_DV)


def _diff_attend(q, q_pos, kv, k_pos, tbl, lam, lam_init, subln_g):
    B, Tq = q.shape[0], q.shape[1]
    G = DIFF_H // DIFF_KV
    qg = q.reshape(B, Tq, DIFF_KV, G, 2 * DIFF_DH) * (DIFF_DH ** -0.5)
    bias = jnp.transpose(tbl.reshape(N_BUCKETS, DIFF_KV, G)[_rel_bucket(q_pos[:, None] - k_pos[None, :])], (0, 2, 3, 1))
    mask = (k_pos[None, :] <= q_pos[:, None])[None, :, None, None, :]
    s1 = jnp.einsum('bqkgd,btkd->bqkgt', qg[..., :DIFF_DH], kv[..., :DIFF_DH]) + bias
    s2 = jnp.einsum('bqkgd,btkd->bqkgt', qg[..., DIFF_DH:], kv[..., DIFF_DH:2 * DIFF_DH]) + bias
    a = _masked_softmax(s1, mask) - lam * _masked_softmax(s2, mask)
    o = jnp.einsum('bqkgt,btkd->bqkgd', a.astype(kv.dtype), kv[..., 2 * DIFF_DH:])
    o = _rms(o, subln_g) * (1.0 - lam_init)
    return o.reshape(B, Tq, DIFF_H * 2 * DIFF_DH)


def _mem_attend(q, mkv):
    B, T = q.shape[0], q.shape[1]
    s = jnp.einsum('bqhd,bmhd->bhqm', q, mkv[..., :MEM_DH]) * (MEM_DH ** -0.5)
    p = jax.nn.softmax(s.astype(jnp.float32), axis=-1)
    o = jnp.einsum('bhqm,bmhd->bqhd', p.astype(mkv.dtype), mkv[..., MEM_DH:])
    return o.reshape(B, T, MEM_H * MEM_DH)


def _merge(x, outs, f, lw):
    o = jnp.stack(outs, axis=2) * jax.nn.silu(f['z'])
    br = jnp.einsum('btnw,nwd->btnd', o, lw['w_branch'])
    h = jnp.sum(jax.nn.sigmoid(f['mg']) * br, axis=2)
    return x + _rms(h @ lw['w_out'], lw['post_norm_g'])


def _sweep(fn, n_blocks):
    out = lax.map(fn, jnp.arange(n_blocks))
    nq, b, qb, w = out.shape
    return jnp.transpose(out, (1, 0, 2, 3)).reshape(b, nq * qb, w)


def _prompt_layer(x, mem, lw):
    B, T, _ = x.shape
    pos = jnp.arange(T)
    f = _features(_rms(x, lw['pre_norm_g']), pos, lw)
    n_qb = T // Q_BLOCK
    kvc = f['nsa_cmp'].reshape(B, T // NSA_BLOCK, NSA_BLOCK, NSA_KV, 2 * NSA_DH).mean(axis=2)
    slc = f['nsa_slc']
    b_ix = jnp.arange(B)[:, None, None, None, None]
    k_ix = jnp.arange(NSA_KV)[None, None, :, None, None]

    def fetch(kpos):
        return slc[b_ix, jnp.clip(kpos, 0, T - 1), k_ix]

    win_pad = jnp.pad(f['nsa_win'], ((0, 0), (NSA_WINDOW, 0), (0, 0), (0, 0)))

    def qslice(a, q0):
        return lax.dynamic_slice_in_dim(a, q0, Q_BLOCK, axis=1)

    def nsa_blk(i):
        q0 = i * Q_BLOCK
        q_pos = q0 + jnp.arange(Q_BLOCK)
        kvw = lax.dynamic_slice_in_dim(win_pad, q0, NSA_WINDOW + Q_BLOCK, axis=1)
        w_pos = q0 - NSA_WINDOW + jnp.arange(NSA_WINDOW + Q_BLOCK)
        return _nsa_attend(qslice(f['nsa_q'], q0), q_pos, qslice(f['nsa_gate'], q0), kvc, fetch, kvw, w_pos, lw['tbl_nsa'])

    def mla_blk(i):
        q0 = i * Q_BLOCK
        q_pos = q0 + jnp.arange(Q_BLOCK)
        return _mla_attend(qslice(f['mla_q_lat'], q0), qslice(f['mla_q_rope'], q0), f['mla_latent'], q_pos, pos, lw['w_mla_uv'])

    def diff_blk(i):
        q0 = i * Q_BLOCK
        q_pos = q0 + jnp.arange(Q_BLOCK)
        return _diff_attend(qslice(f['diff_q'], q0), q_pos, f['diff_kv'], pos, lw['tbl_diff'], lw['lam'], lw['lam_init'], lw['diff_subln_g'])

    o_nsa = _sweep(nsa_blk, n_qb)
    o_mla = _sweep(mla_blk, n_qb)
    o_diff = _sweep(diff_blk, n_qb)
    mkv = (_rms(mem, lw['mem_norm_g']) @ lw['w_mem_kv']).reshape(B, N_MEM, MEM_H, 2 * MEM_DH)
    o_mem = _mem_attend(f['mem_q'], mkv)
    y = _merge(x, [o_nsa, o_mla, o_diff, o_mem], f, lw)
    win_keep = min(NSA_WINDOW, T)
    return y, (f['nsa_cmp'], f['nsa_slc'], f['nsa_win'][:, T - win_keep:], f['mla_latent'], f['diff_kv'], mkv)


def _sample_layer(x, c_cmp, c_slc, c_win, c_lat, c_diff, c_mem, page_table, lw):
    B, T, _ = x.shape
    n_pages = page_table.shape[1]
    past = n_pages * PAGE_SIZE
    q_pos = past + jnp.arange(T)
    k_pos = jnp.arange(past + T)
    f = _features(_rms(x, lw['pre_norm_g']), q_pos, lw)

    def gather_all(pool):
        g = pool[page_table]
        return g.reshape((B, past) + pool.shape[2:])

    cmp_all = jnp.concatenate([gather_all(c_cmp), f['nsa_cmp']], axis=1)
    L = past + T
    nb = -(-L // NSA_BLOCK)
    cmp_all = jnp.pad(cmp_all, ((0, 0), (0, nb * NSA_BLOCK - L), (0, 0), (0, 0)))
    kvc = cmp_all.reshape(B, nb, NSA_BLOCK, NSA_KV, 2 * NSA_DH).mean(axis=2)
    new_slc = f['nsa_slc']
    b_ix = jnp.arange(B)[:, None, None, None, None]
    k_ix = jnp.arange(NSA_KV)[None, None, :, None, None]

    def fetch(kpos):
        pp = jnp.clip(kpos, 0, past - 1)
        phys = page_table[b_ix, pp // PAGE_SIZE]
        from_past = c_slc[phys, pp % PAGE_SIZE, k_ix]
        from_new = new_slc[b_ix, jnp.clip(kpos - past, 0, T - 1), k_ix]
        return jnp.where((kpos < past)[..., None], from_past, from_new)

    wbuf = c_win.shape[1]
    kvw = jnp.concatenate([c_win, f['nsa_win']], axis=1)
    w_pos = past - wbuf + jnp.arange(wbuf + T)
    o_nsa = _nsa_attend(f['nsa_q'], q_pos, f['nsa_gate'], kvc, fetch, kvw, w_pos, lw['tbl_nsa'])
    lat_all = jnp.concatenate([gather_all(c_lat), f['mla_latent']], axis=1)
    o_mla = _mla_attend(f['mla_q_lat'], f['mla_q_rope'], lat_all, q_pos, k_pos, lw['w_mla_uv'])
    diff_all = jnp.concatenate([gather_all(c_diff), f['diff_kv']], axis=1)
    o_diff = _diff_attend(f['diff_q'], q_pos, diff_all, k_pos, lw['tbl_diff'], lw['lam'], lw['lam_init'], lw['diff_subln_g'])
    o_mem = _mem_attend(f['mem_q'], c_mem)
    y = _merge(x, [o_nsa, o_mla, o_diff, o_mem], f, lw)
    return y, (f['nsa_cmp'], f['nsa_slc'], kvw[:, T:], f['mla_latent'], f['diff_kv'])


def setup_inputs(seed: int = 0) -> dict:
    key = jax.random.key(seed)
    ks = jax.random.split(key, 32)
    n_pages = PAST_LEN // PAGE_SIZE
    n_pool = (DEC_BATCH * n_pages * 5) // 4
    win_buf = min(NSA_WINDOW, PAST_LEN)

    def nrm(k, shape, s=1.0):
        return s * jax.random.normal(k, shape, jnp.float32)

    def gain(k, shape):
        return 1.0 + 0.02 * jax.random.normal(k, shape, jnp.float32)

    perm = jax.random.permutation(ks[9], n_pool)[:DEC_BATCH * n_pages]
    return {
        'x_prompt': nrm(ks[0], (BATCH, SEQ, D_MODEL)),
        'x_sample': nrm(ks[1], (DEC_BATCH, DEC_SEQ, D_MODEL)),
        'cache_nsa_cmp_kv': nrm(ks[2], (DEPTH, n_pool, PAGE_SIZE, NSA_KV, 2 * NSA_DH)),
        'cache_nsa_slc_kv': nrm(ks[3], (DEPTH, n_pool, PAGE_SIZE, NSA_KV, 2 * NSA_DH)),
        'cache_nsa_win_kv': nrm(ks[4], (DEPTH, DEC_BATCH, win_buf, NSA_KV, 2 * NSA_DH)),
        'cache_mla_latent': nrm(ks[5], (DEPTH, n_pool, PAGE_SIZE, MLA_KV_RANK + MLA_ROPE)),
        'cache_diff_kv': nrm(ks[6], (DEPTH, n_pool, PAGE_SIZE, DIFF_KV, 4 * DIFF_DH)),
        'cache_mem_kv': nrm(ks[7], (DEPTH, DEC_BATCH, N_MEM, MEM_H, 2 * MEM_DH)),
        'page_table': perm.reshape(DEC_BATCH, n_pages).astype(jnp.int32),
        'mem_prompt': nrm(ks[8], (BATCH, N_MEM, D_MODEL)),
        'rel_bias': nrm(ks[10], (N_BUCKETS, BIAS_H), 0.2),
        'pre_norm_g': gain(ks[11], (DEPTH, D_MODEL)),
        'post_norm_g': gain(ks[12], (DEPTH, D_MODEL)),
        'w_in': nrm(ks[13], (DEPTH, D_MODEL, D_IN), D_MODEL ** -0.5),
        'mla_q_norm_g': gain(ks[14], (DEPTH, MLA_Q_RANK)),
        'w_mla_uq': nrm(ks[15], (DEPTH, MLA_Q_RANK, MLA_H * (MLA_NOPE + MLA_ROPE)), MLA_Q_RANK ** -0.5),
        'mla_kv_norm_g': gain(ks[16], (DEPTH, MLA_KV_RANK)),
        'w_mla_uk': nrm(ks[17], (DEPTH, MLA_KV_RANK, MLA_H * MLA_NOPE), MLA_KV_RANK ** -0.5),
        'w_mla_uv': nrm(ks[18], (DEPTH, MLA_KV_RANK, MLA_H * MLA_DV), MLA_KV_RANK ** -0.5),
        'diff_lambda_q1': nrm(ks[19], (DEPTH, DIFF_DH), 0.1),
        'diff_lambda_k1': nrm(ks[20], (DEPTH, DIFF_DH), 0.1),
        'diff_lambda_q2': nrm(ks[21], (DEPTH, DIFF_DH), 0.1),
        'diff_lambda_k2': nrm(ks[22], (DEPTH, DIFF_DH), 0.1),
        'diff_subln_g': gain(ks[23], (DEPTH, 2 * DIFF_DH)),
        'mem_norm_g': gain(ks[24], (DEPTH, D_MODEL)),
        'w_mem_kv': nrm(ks[25], (DEPTH, D_MODEL, MEM_H * 2 * MEM_DH), D_MODEL ** -0.5),
        'w_branch': nrm(ks[26], (DEPTH, N_BRANCH, BRANCH_W, D_MODEL), BRANCH_W ** -0.5),
        'w_out': nrm(ks[27], (DEPTH, D_MODEL, D_MODEL), D_MODEL ** -0.5),
    }


def reference(x_prompt, x_sample, cache_nsa_cmp_kv, cache_nsa_slc_kv, cache_nsa_win_kv, cache_mla_latent,
              cache_diff_kv, cache_mem_kv, page_table, mem_prompt, rel_bias, pre_norm_g, post_norm_g, w_in,
              mla_q_norm_g, w_mla_uq, mla_kv_norm_g, w_mla_uk, w_mla_uv, diff_lambda_q1, diff_lambda_k1,
              diff_lambda_q2, diff_lambda_k2, diff_subln_g, mem_norm_g, w_mem_kv, w_branch, w_out):
    tbl_nsa = rel_bias[:, :NSA_H]
    tbl_diff = rel_bias[:, NSA_H:]
    xp, xs = x_prompt, x_sample
    p_st = [[] for _ in range(6)]
    s_st = [[] for _ in range(5)]
    for l in range(DEPTH):
        lam_init = 0.8 - 0.6 * math.exp(-0.3 * l)
        lam = (jnp.exp(jnp.sum(diff_lambda_q1[l].astype(jnp.float32) * diff_lambda_k1[l].astype(jnp.float32)))
               - jnp.exp(jnp.sum(diff_lambda_q2[l].astype(jnp.float32) * diff_lambda_k2[l].astype(jnp.float32)))
               + lam_init)
        lw = {
            'pre_norm_g': pre_norm_g[l], 'post_norm_g': post_norm_g[l], 'w_in': w_in[l],
            'mla_q_norm_g': mla_q_norm_g[l], 'w_mla_uq': w_mla_uq[l], 'mla_kv_norm_g': mla_kv_norm_g[l],
            'w_mla_uk': w_mla_uk[l], 'w_mla_uv': w_mla_uv[l], 'diff_subln_g': diff_subln_g[l],
            'mem_norm_g': mem_norm_g[l], 'w_mem_kv': w_mem_kv[l], 'w_branch': w_branch[l], 'w_out': w_out[l],
            'lam': lam, 'lam_init': lam_init, 'tbl_nsa': tbl_nsa, 'tbl_diff': tbl_diff,
        }
        xp, pst = _prompt_layer(xp, mem_prompt, lw)
        xs, sst = _sample_layer(xs, cache_nsa_cmp_kv[l], cache_nsa_slc_kv[l], cache_nsa_win_kv[l], cache_mla_latent[l],
                                cache_diff_kv[l], cache_mem_kv[l], page_table, lw)
        for i in range(6):
            p_st[i].append(pst[i])
        for i in range(5):
            s_st[i].append(sst[i])
    p_nsa_cmp, p_nsa_slc, p_nsa_win, p_mla_latent, p_diff_kv, p_mem_kv = [jnp.stack(a, axis=0) for a in p_st]
    s_nsa_cmp, s_nsa_slc, s_nsa_win, s_mla_latent, s_diff_kv = [jnp.stack(a, axis=0) for a in s_st]
    return (xp, xs, p_nsa_cmp, s_nsa_cmp, p_nsa_slc, s_nsa_slc, p_nsa_win, s_nsa_win,
            p_mla_latent, s_mla_latent, p_diff_kv, s_diff_kv, p_mem_kv)
```

```python
import functools
import math

import numpy as np
import jax
import jax.numpy as jnp
from jax import lax
from jax.experimental import pallas as pl
from jax.experimental.pallas import tpu as pltpu

F32 = jnp.float32
BF16 = jnp.bfloat16

D_MODEL = 1024
PAGE = 128
BRANCH_W = 512
N_BRANCH = 4
NSA_H = 8
NSA_DH = 64
NSA_BLOCK = 64
NSA_TOPN = 16
NSA_WINDOW = 512
MLA_H = 8
MLA_Q_RANK = 256
MLA_KV_RANK = 128
MLA_NOPE = 64
MLA_ROPE = 32
MLA_DV = 64
MLA_LAT = MLA_KV_RANK + MLA_ROPE
ROPE_THETA = 10000.0
DIFF_H = 4
DIFF_DH = 64
N_MEM = 256
MEM_H = 4
MEM_DH = 128
N_BUCKETS = 32
MAX_DISTANCE = 128
BIAS_H = NSA_H + DIFF_H
EPS = 1e-6
NEG = -1e30
TINY = float(np.finfo(np.float32).tiny)

VMEM_LIMIT = 56 * 1024 * 1024

SAMPLE_ROWS = 8
PAGES_PER_STEP = 16

_US_NSA_Q, _US_DIFF_Q, _US_MEM_Q, _US_CQ, _US_DIFF_KV = 0, 512, 1024, 1536, 1792
_US_CMP, _US_SLC, _US_WIN, _US_CKV, _US_MISC, _US_W = 2048, 2176, 2304, 2432, 2560, 2688
_MISC_KR, _MISC_KRS = 24, 56


def _bucket_thresholds():
    n = np.arange(0, 4 * MAX_DISTANCE, dtype=np.int64)
    exact = N_BUCKETS // 2
    nf = np.maximum(n, exact).astype(np.float32)
    large = exact + (np.log(nf / np.float32(exact)) / np.float32(math.log(MAX_DISTANCE / exact))
                     * np.float32(N_BUCKETS - exact)).astype(np.int32)
    bucket = np.where(n < exact, n, np.minimum(large, N_BUCKETS - 1))
    return [int(np.argmax(bucket >= b)) for b in range(N_BUCKETS)]


_THR = _bucket_thresholds()


def _cparams(sem):
    return pltpu.CompilerParams(dimension_semantics=sem, vmem_limit_bytes=VMEM_LIMIT)


def _resident(shape):
    nd = len(shape)
    return pl.BlockSpec(shape, lambda *a: (0,) * nd, pipeline_mode=pl.Buffered(1))


def _dot(a, b):
    return jnp.dot(a, b, preferred_element_type=F32)


def _dot_nt(a, b):
    return lax.dot_general(a, b, (((1,), (1,)), ((), ())), preferred_element_type=F32)


def _rms_rows(x, g):
    return x * lax.rsqrt(jnp.mean(x * x, axis=-1, keepdims=True) + EPS) * g


def _rms_matmul_kernel(x_ref, g_ref, w_ref, o_ref):
    y = _rms_rows(x_ref[...], g_ref[...])
    o_ref[...] = _dot(y.astype(BF16), w_ref[...]).astype(o_ref.dtype)


def _rms_matmul(x, g, w, out_dtype, tm, tn, name):
    m, k = x.shape
    n = w.shape[1]
    return pl.pallas_call(
        _rms_matmul_kernel,
        grid=(n // tn, m // tm),
        in_specs=[pl.BlockSpec((tm, k), lambda j, i: (i, 0)),
                  pl.BlockSpec((1, k), lambda j, i: (0, 0)),
                  pl.BlockSpec((k, tn), lambda j, i: (0, j))],
        out_specs=pl.BlockSpec((tm, tn), lambda j, i: (i, j)),
        out_shape=jax.ShapeDtypeStruct((m, n), out_dtype),
        compiler_params=_cparams(("parallel", "parallel")),
        name=name,
    )(x, g.reshape(1, k), w)


def _bias_expand_kernel(tbl_ref, d_ref, o_ref, *, h0):
    h = pl.program_id(0) + h0
    d = d_ref[...]
    out = jnp.full(d.shape, tbl_ref[0, h], F32)
    for b in range(1, N_BUCKETS):
        out = jnp.where(d >= _THR[b], tbl_ref[b, h], out)
    o_ref[0] = out - tbl_ref[N_BUCKETS - 1, h]


def _bias_expand(tbl, dist, h0, nh, name):
    r, c = dist.shape
    tr = min(r, 512)
    return pl.pallas_call(
        functools.partial(_bias_expand_kernel, h0=h0),
        grid=(nh, r // tr),
        in_specs=[pl.BlockSpec(memory_space=pltpu.SMEM),
                  pl.BlockSpec((tr, c), lambda h, i: (i, 0))],
        out_specs=pl.BlockSpec((1, tr, c), lambda h, i: (h, i, 0)),
        out_shape=jax.ShapeDtypeStruct((nh, r, c), F32),
        compiler_params=_cparams(("parallel", "parallel")),
        name=name,
    )(tbl, dist)


def _mla_prep_kernel(cq_ref, ckv_ref, misc_ref, cos_ref, sin_ref, gq_ref, gkv_ref,
                     wn_ref, wr_ref, wrs_ref, wuk_ref, q_ref, lat_ref):
    scale = (MLA_NOPE + MLA_ROPE) ** -0.5
    cqn = _rms_rows(cq_ref[...], gq_ref[...]).astype(BF16)
    qn = _dot(cqn, wn_ref[...])
    qr = _dot(cqn, wr_ref[...])
    qrs = _dot(cqn, wrs_ref[...])
    cos = cos_ref[...]
    sin = sin_ref[...]
    for h in range(MLA_H):
        ql = _dot(qn[:, h * MLA_NOPE:(h + 1) * MLA_NOPE].astype(BF16), wuk_ref[h])
        q_ref[h, :, 0:MLA_KV_RANK] = (ql * scale).astype(BF16)
        sl = slice(h * MLA_ROPE, (h + 1) * MLA_ROPE)
        rope = qr[:, sl] * cos + qrs[:, sl] * sin
        q_ref[h, :, MLA_KV_RANK:MLA_LAT] = (rope * scale).astype(BF16)
    lat_ref[:, 0:MLA_KV_RANK] = _rms_rows(ckv_ref[...], gkv_ref[...])
    misc = misc_ref[...]
    kr = misc[:, _MISC_KR:_MISC_KR + MLA_ROPE]
    krs = misc[:, _MISC_KRS:_MISC_KRS + MLA_ROPE]
    lat_ref[:, MLA_KV_RANK:MLA_LAT] = kr * cos + krs * sin


def _mla_prep(us, cos, sin, lw, tm, name):
    m = us.shape[0]
    col = lambda off, w: pl.BlockSpec((tm, w), lambda i: (i, off // w))
    return pl.pallas_call(
        _mla_prep_kernel,
        grid=(m // tm,),
        in_specs=[col(_US_CQ, MLA_Q_RANK), col(_US_CKV, MLA_KV_RANK), col(_US_MISC, 128),
                  pl.BlockSpec((tm, MLA_ROPE), lambda i: (i, 0)),
                  pl.BlockSpec((tm, MLA_ROPE), lambda i: (i, 0)),
                  _resident((1, MLA_Q_RANK)), _resident((1, MLA_KV_RANK)),
                  _resident((MLA_Q_RANK, MLA_H * MLA_NOPE)),
                  _resident((MLA_Q_RANK, MLA_H * MLA_ROPE)),
                  _resident((MLA_Q_RANK, MLA_H * MLA_ROPE)),
                  _resident((MLA_H, MLA_NOPE, MLA_KV_RANK))],
        out_specs=[pl.BlockSpec((MLA_H, tm, MLA_LAT), lambda i: (0, i, 0)),
                   pl.BlockSpec((tm, MLA_LAT), lambda i: (i, 0))],
        out_shape=[jax.ShapeDtypeStruct((MLA_H, m, MLA_LAT), BF16),
                   jax.ShapeDtypeStruct((m, MLA_LAT), F32)],
        compiler_params=_cparams(("parallel",)),
        name=name,
    )(us, us, us, cos, sin, lw['gq'], lw['gkv'], lw['w_uq_n'], lw['w_uq_r'], lw['w_uq_rs'], lw['w_ukT'])


def _pad_heads_128(q, n_heads, scale):
    r = q.shape[0]
    lane = lax.broadcasted_iota(jnp.int32, (r, 128), 1)
    parts = []
    for h in range(n_heads):
        slab = q[:, (h // 2) * 128:(h // 2 + 1) * 128]
        if h % 2 == 1:
            slab = pltpu.roll(slab, 64, 1)
        parts.append(jnp.where(lane < 64, slab * scale, 0.0).astype(BF16))
    return jnp.concatenate(parts, axis=0)


def _flash_init(s, v, m_ref, l_ref, acc_ref, w=None):
    m = jnp.max(s, axis=-1, keepdims=True)
    p = jnp.exp(s - m)
    if w is not None:
        p = p * w
    m_ref[...] = m
    l_ref[...] = jnp.sum(p, axis=-1, keepdims=True)
    acc_ref[...] = _dot(p.astype(BF16), v)


def _flash_step(s, v, m_ref, l_ref, acc_ref, w=None):
    m_prev = m_ref[...]
    m_new = jnp.maximum(m_prev, jnp.max(s, axis=-1, keepdims=True))
    a = jnp.exp(m_prev - m_new)
    p = jnp.exp(s - m_new)
    if w is not None:
        p = p * w
    l_ref[...] = a * l_ref[...] + jnp.sum(p, axis=-1, keepdims=True)
    acc_ref[...] = a * acc_ref[...] + _dot(p.astype(BF16), v)
    m_ref[...] = m_new


def _heads3(x, h):
    return x.reshape(h, x.shape[0] // h, x.shape[1])


def _sigmoid(x):
    return 1.0 / (1.0 + jnp.exp(-x))


def _nsa_cmp_kernel(q_ref, misc_ref, kvc_ref, bias_ref, qpos_ref, oc_ref, cnt_ref, *, nb_true):
    r = q_ref.shape[0]
    nbk = kvc_ref.shape[-2]
    kv = kvc_ref[...].reshape(nbk, 128).astype(BF16)
    qp = _pad_heads_128(q_ref[...], NSA_H, NSA_DH ** -0.5)
    qpos = qpos_ref[...]
    blk = lax.broadcasted_iota(jnp.int32, (r, nbk), 1)
    valid = jnp.logical_and(blk * NSA_BLOCK + (NSA_BLOCK - 1) <= qpos, blk < nb_true)
    s = _heads3(_dot_nt(qp, kv), NSA_H) + bias_ref[...]
    s = jnp.where(valid[None], s, NEG)
    m = jnp.max(s, axis=-1, keepdims=True)
    e = jnp.where(valid[None], jnp.exp(s - m), 0.0)
    p = e / jnp.maximum(jnp.sum(e, axis=-1, keepdims=True), TINY)
    o = _dot(p.reshape(NSA_H * r, nbk).astype(BF16), kv)
    g0 = _sigmoid(misc_ref[:, 0:NSA_H])
    pieces = [o[h * r:(h + 1) * r, NSA_DH:] * g0[:, h:h + 1] for h in range(NSA_H)]
    oc_ref[...] = jnp.concatenate(pieces, axis=1)
    work = jnp.where(valid, jnp.sum(p, axis=0), -1.0)
    cur = lax.shift_right_arithmetic(qpos, int(math.log2(NSA_BLOCK)))
    cnt = jnp.where(blk == cur, 1.0, 0.0)
    blkf = blk.astype(F32)
    for _ in range(min(NSA_TOPN, nb_true)):
        mx = jnp.max(work, axis=-1, keepdims=True)
        idx = jnp.min(jnp.where(work == mx, blkf, float(nbk)), axis=-1, keepdims=True)
        sel = blkf == idx
        cnt = cnt + jnp.where(jnp.logical_and(sel, mx >= 0.0), 1.0, 0.0)
        work = jnp.where(sel, -2.0, work)
    cnt_ref[...] = cnt.astype(BF16)


def _nsa_cmp(us, kvc, bias_c, qpos, nb_true, rows, per_seq, name):
    m = us.shape[0]
    nbk = kvc.shape[-2]
    col = lambda off, w: pl.BlockSpec((rows, w), lambda i: (i, off // w))
    if per_seq:
        kvc_spec = pl.BlockSpec((1, nbk, 128), lambda i: (i, 0, 0))
        bias_spec = _resident((NSA_H, rows, nbk))
    else:
        kvc_spec = _resident((1, nbk, 128))
        bias_spec = pl.BlockSpec((NSA_H, rows, nbk), lambda i: (0, i, 0))
    return pl.pallas_call(
        functools.partial(_nsa_cmp_kernel, nb_true=nb_true),
        grid=(m // rows,),
        in_specs=[col(_US_NSA_Q, 512), col(_US_MISC, 128), kvc_spec, bias_spec,
                  pl.BlockSpec((rows, 1), lambda i: (i, 0))],
        out_specs=[pl.BlockSpec((rows, 512), lambda i: (i, 0)),
                   pl.BlockSpec((rows, nbk), lambda i: (i, 0))],
        out_shape=[jax.ShapeDtypeStruct((m, 512), F32), jax.ShapeDtypeStruct((m, nbk), BF16)],
        compiler_params=_cparams(("parallel",)),
        name=name,
    )(us, us, kvc, bias_c, qpos)


def _pool_kernel(x_ref, o_ref):
    x = x_ref[...]
    nb = x.shape[0] // NSA_BLOCK
    o_ref[0] = jnp.sum(x.reshape(nb, NSA_BLOCK, 128), axis=1) * (1.0 / NSA_BLOCK)


def _pool_prompt(us, name):
    t = us.shape[0]
    rows = min(t, 2048)
    nb = t // NSA_BLOCK
    return pl.pallas_call(
        _pool_kernel,
        grid=(t // rows,),
        in_specs=[pl.BlockSpec((rows, 128), lambda i: (i, _US_CMP // 128))],
        out_specs=pl.BlockSpec((1, rows // NSA_BLOCK, 128), lambda i: (0, i, 0)),
        out_shape=jax.ShapeDtypeStruct((1, nb, 128), F32),
        compiler_params=_cparams(("parallel",)),
        name=name,
    )(us)


def _nsa_prompt_kernel(q_ref, misc_ref, oc_ref, cnt_ref, slc_ref, win_ref, bt_ref, e3_ref, o_ref,
                       ms, ls, accs, mw, lw, accw):
    tq = q_ref.shape[0]
    qi = pl.program_id(0)
    qp = _pad_heads_128(q_ref[...], NSA_H, NSA_DH ** -0.5)
    cnt = cnt_ref[...]
    row = lax.broadcasted_iota(jnp.int32, (tq, tq), 0)
    colk = lax.broadcasted_iota(jnp.int32, (tq, tq), 1)

    def tile(ref, j):
        return ref[pl.ds(pl.multiple_of(j * tq, tq), tq), :]

    def slc_scores(j, bias, mask):
        kv = tile(slc_ref, j)
        w = _dot(cnt, e3_ref[j])
        ok = w > 0.0 if mask is None else jnp.logical_and(w > 0.0, mask)
        s = _heads3(_dot_nt(qp, kv), NSA_H)
        if bias is not None:
            s = s + bias
        s = jnp.where(ok[None], s, NEG).reshape(NSA_H * tq, tq)
        wfull = jnp.broadcast_to(w[None], (NSA_H, tq, tq)).reshape(NSA_H * tq, tq)
        return s, kv, wfull

    def win_scores(j, bias, mask):
        kv = tile(win_ref, j)
        s = _heads3(_dot_nt(qp, kv), NSA_H)
        if bias is not None:
            s = s + bias
        if mask is not None:
            s = jnp.where(mask[None], s, NEG)
        return s.reshape(NSA_H * tq, tq), kv

    causal = colk <= row
    s, kv, w = slc_scores(qi, bt_ref[:, 0], causal)
    _flash_init(s, kv, ms, ls, accs, w)
    s, kv = win_scores(qi, bt_ref[:, 0], causal)
    _flash_init(s, kv, mw, lw, accw)

    @pl.when(qi >= 1)
    def _():
        s, kv, w = slc_scores(qi - 1, bt_ref[:, 1], None)
        _flash_step(s, kv, ms, ls, accs, w)
        s, kv = win_scores(qi - 1, bt_ref[:, 1], None)
        _flash_step(s, kv, mw, lw, accw)

    @pl.when(qi >= 2)
    def _():
        s, kv, w = slc_scores(qi - 2, None, None)
        _flash_step(s, kv, ms, ls, accs, w)
        s, kv = win_scores(qi - 2, None, colk >= row)
        _flash_step(s, kv, mw, lw, accw)

    def far(j, c):
        s, kv, w = slc_scores(j, None, None)
        _flash_step(s, kv, ms, ls, accs, w)
        return c

    lax.fori_loop(0, jnp.maximum(qi - 2, 0), far, 0)

    g = _sigmoid(misc_ref[:, 0:3 * NSA_H])
    os_ = accs[...] / ls[...]
    ow = accw[...] / lw[...]
    oc = oc_ref[...]
    pieces = []
    for h in range(NSA_H):
        sl = slice(h * tq, (h + 1) * tq)
        pieces.append(oc[:, h * NSA_DH:(h + 1) * NSA_DH]
                      + g[:, NSA_H + h:NSA_H + h + 1] * os_[sl, NSA_DH:]
                      + g[:, 2 * NSA_H + h:2 * NSA_H + h + 1] * ow[sl, NSA_DH:])
    o_ref[...] = jnp.concatenate(pieces, axis=1)


def _nsa_prompt(us, oc, cnt, slc_bf, win_bf, bt, e3, tq, name):
    t = us.shape[0]
    nb = cnt.shape[1]
    hr = NSA_H * tq
    col = lambda off, w: pl.BlockSpec((tq, w), lambda i: (i, off // w))
    return pl.pallas_call(
        _nsa_prompt_kernel,
        grid=(t // tq,),
        in_specs=[col(_US_NSA_Q, 512), col(_US_MISC, 128),
                  pl.BlockSpec((tq, 512), lambda i: (i, 0)),
                  pl.BlockSpec((tq, nb), lambda i: (i, 0)),
                  _resident((t, 128)), _resident((t, 128)),
                  _resident((NSA_H, 2, tq, tq)), _resident(e3.shape)],
        out_specs=pl.BlockSpec((tq, 512), lambda i: (i, 0)),
        out_shape=jax.ShapeDtypeStruct((t, 512), F32),
        scratch_shapes=[pltpu.VMEM((hr, 1), F32), pltpu.VMEM((hr, 1), F32), pltpu.VMEM((hr, 128), F32),
                        pltpu.VMEM((hr, 1), F32), pltpu.VMEM((hr, 1), F32), pltpu.VMEM((hr, 128), F32)],
        compiler_params=_cparams(("parallel",)),
        name=name,
    )(us, us, oc, cnt, slc_bf, win_bf, bt, e3)


def _mla_finish(acc, l, wuv_ref, o_ref, rows):
    o_lat = acc / l
    pieces = [_dot(o_lat[h * rows:(h + 1) * rows].astype(BF16), wuv_ref[h]) for h in range(MLA_H)]
    o_ref[...] = jnp.concatenate(pieces, axis=1)


def _mla_prompt_kernel(q_ref, lat_ref, wuv_ref, o_ref, m_ref, l_ref, acc_ref):
    tq = q_ref.shape[1]
    qi = pl.program_id(0)
    q = q_ref[...].reshape(MLA_H * tq, MLA_LAT)
    row = lax.broadcasted_iota(jnp.int32, (tq, tq), 0)
    colk = lax.broadcasted_iota(jnp.int32, (tq, tq), 1)

    def tile(j):
        return lat_ref[pl.ds(pl.multiple_of(j * tq, tq), tq), :]

    k = tile(qi)
    s = _heads3(_dot_nt(q, k), MLA_H)
    s = jnp.where((colk <= row)[None], s, NEG).reshape(MLA_H * tq, tq)
    _flash_init(s, k[:, :MLA_KV_RANK], m_ref, l_ref, acc_ref)

    def body(j, c):
        k = tile(j)
        _flash_step(_dot_nt(q, k), k[:, :MLA_KV_RANK], m_ref, l_ref, acc_ref)
        return c

    lax.fori_loop(0, qi, body, 0)
    _mla_finish(acc_ref[...], l_ref[...], wuv_ref, o_ref, tq)


def _mla_prompt(q_mla, lat_bf, wuv, tq, name):
    t = lat_bf.shape[0]
    hr = MLA_H * tq
    return pl.pallas_call(
        _mla_prompt_kernel,
        grid=(t // tq,),
        in_specs=[pl.BlockSpec((MLA_H, tq, MLA_LAT), lambda i: (0, i, 0)),
                  _resident((t, MLA_LAT)), _resident((MLA_H, MLA_KV_RANK, MLA_DV))],
        out_specs=pl.BlockSpec((tq, 512), lambda i: (i, 0)),
        out_shape=jax.ShapeDtypeStruct((t, 512), F32),
        scratch_shapes=[pltpu.VMEM((hr, 1), F32), pltpu.VMEM((hr, 1), F32),
                        pltpu.VMEM((hr, MLA_KV_RANK), F32)],
        compiler_params=_cparams(("parallel",)),
        name=name,
    )(q_mla, lat_bf, wuv)


def _diff_queries(q):
    r = q.shape[0]
    lane = lax.broadcasted_iota(jnp.int32, (r, 128), 1)
    qs = q * (DIFF_DH ** -0.5)
    m1 = [jnp.where(lane < DIFF_DH, qs[:, h * 128:(h + 1) * 128], 0.0).astype(BF16) for h in range(DIFF_H)]
    m2 = [jnp.where(lane >= DIFF_DH, qs[:, h * 128:(h + 1) * 128], 0.0).astype(BF16) for h in range(DIFF_H)]
    return jnp.concatenate(m1 + m2, axis=0)


def _diff_lambda(lp_ref, lam_init):
    lp = lp_ref[...]
    return (jnp.exp(jnp.sum(lp[0:1] * lp[1:2], axis=-1, keepdims=True))
            - jnp.exp(jnp.sum(lp[2:3] * lp[3:4], axis=-1, keepdims=True)) + lam_init)


def _diff_finish(acc, l, lp_ref, g_ref, o_ref, rows, lam_init):
    lam = _diff_lambda(lp_ref, lam_init)
    o = acc / l
    half = DIFF_H * rows
    d = o[:half] - lam * o[half:]
    y = _rms_rows(d, g_ref[...]) * (1.0 - lam_init)
    o_ref[...] = jnp.concatenate([y[h * rows:(h + 1) * rows] for h in range(DIFF_H)], axis=1)


def _diff_prompt_kernel(q_ref, kv_ref, bt_ref, lp_ref, g_ref, o_ref, m_ref, l_ref, acc_ref, *, lam_init):
    tq = q_ref.shape[0]
    qi = pl.program_id(0)
    q = _diff_queries(q_ref[...])
    row = lax.broadcasted_iota(jnp.int32, (tq, tq), 0)
    colk = lax.broadcasted_iota(jnp.int32, (tq, tq), 1)

    def tile(j):
        return kv_ref[pl.ds(pl.multiple_of(j * tq, tq), tq), :]

    def scores(kv, bias, mask):
        s = _heads3(_dot_nt(q, kv[:, :2 * DIFF_DH]), 2 * DIFF_H)
        if bias is not None:
            s = s + jnp.concatenate([bias, bias], axis=0)
        if mask is not None:
            s = jnp.where(mask[None], s, NEG)
        return s.reshape(2 * DIFF_H * tq, tq)

    kv = tile(qi)
    _flash_init(scores(kv, bt_ref[:, 0], colk <= row), kv[:, 2 * DIFF_DH:], m_ref, l_ref, acc_ref)

    @pl.when(qi >= 1)
    def _():
        kv = tile(qi - 1)
        _flash_step(scores(kv, bt_ref[:, 1], None), kv[:, 2 * DIFF_DH:], m_ref, l_ref, acc_ref)

    def far(j, c):
        kv = tile(j)
        _flash_step(scores(kv, None, None), kv[:, 2 * DIFF_DH:], m_ref, l_ref, acc_ref)
        return c

    lax.fori_loop(0, jnp.maximum(qi - 1, 0), far, 0)
    _diff_finish(acc_ref[...], l_ref[...], lp_ref, g_ref, o_ref, tq, lam_init)


def _diff_prompt(us, kv_bf, bt, lp, g, lam_init, tq, name):
    t = us.shape[0]
    hr = 2 * DIFF_H * tq
    return pl.pallas_call(
        functools.partial(_diff_prompt_kernel, lam_init=lam_init),
        grid=(t // tq,),
        in_specs=[pl.BlockSpec((tq, 512), lambda i: (i, _US_DIFF_Q // 512)),
                  _resident((t, 4 * DIFF_DH)), _resident((DIFF_H, 2, tq, tq)),
                  _resident((4, DIFF_DH)), _resident((1, 2 * DIFF_DH))],
        out_specs=pl.BlockSpec((tq, 512), lambda i: (i, 0)),
        out_shape=jax.ShapeDtypeStruct((t, 512), F32),
        scratch_shapes=[pltpu.VMEM((hr, 1), F32), pltpu.VMEM((hr, 1), F32),
                        pltpu.VMEM((hr, 2 * DIFF_DH), F32)],
        compiler_params=_cparams(("parallel",)),
        name=name,
    )(us, kv_bf, bt, lp, g)


def _mem_kernel(q_ref, mkv_ref, o_ref):
    q = q_ref[...]
    mkv = mkv_ref[...].reshape(N_MEM, MEM_H * 2 * MEM_DH).astype(BF16)
    outs = []
    for h in range(MEM_H):
        qh = q[:, h * MEM_DH:(h + 1) * MEM_DH].astype(BF16)
        k = mkv[:, h * 2 * MEM_DH:h * 2 * MEM_DH + MEM_DH]
        v = mkv[:, h * 2 * MEM_DH + MEM_DH:(h + 1) * 2 * MEM_DH]
        s = _dot_nt(qh, k) * (MEM_DH ** -0.5)
        m = jnp.max(s, axis=-1, keepdims=True)
        e = jnp.exp(s - m)
        p = e / jnp.sum(e, axis=-1, keepdims=True)
        outs.append(_dot(p.astype(BF16), v))
    o_ref[...] = jnp.concatenate(outs, axis=1)


def _mem_attend(us, mkv, rows, per_seq, name):
    m = us.shape[0]
    w = MEM_H * 2 * MEM_DH
    mkv_spec = (pl.BlockSpec((1, N_MEM, w), lambda i: (i, 0, 0)) if per_seq else _resident((1, N_MEM, w)))
    return pl.pallas_call(
        _mem_kernel,
        grid=(m // rows,),
        in_specs=[pl.BlockSpec((rows, 512), lambda i: (i, _US_MEM_Q // 512)), mkv_spec],
        out_specs=pl.BlockSpec((rows, 512), lambda i: (i, 0)),
        out_shape=jax.ShapeDtypeStruct((m, 512), F32),
        compiler_params=_cparams(("parallel",)),
        name=name,
    )(us, mkv)


def _merge_kernel(x_ref, o0_ref, o1_ref, o2_ref, o3_ref, z_ref, mg_ref, wb_ref, wo_ref, g_ref, y_ref):
    h = None
    for n, o_ref in enumerate((o0_ref, o1_ref, o2_ref, o3_ref)):
        z = z_ref[:, n * BRANCH_W:(n + 1) * BRANCH_W].astype(F32)
        gated = o_ref[...] * (z * _sigmoid(z))
        br = _dot(gated.astype(BF16), wb_ref[n])
        term = _sigmoid(mg_ref[:, n * D_MODEL:(n + 1) * D_MODEL].astype(F32)) * br
        h = term if h is None else h + term
    y = _dot(h.astype(BF16), wo_ref[...])
    y_ref[...] = x_ref[...] + _rms_rows(y, g_ref[...])


def _merge(x, outs, ub, lw, tm, name):
    m = x.shape[0]
    row = lambda w: pl.BlockSpec((tm, w), lambda i: (i, 0))
    return pl.pallas_call(
        _merge_kernel,
        grid=(m // tm,),
        in_specs=[row(D_MODEL), row(512), row(512), row(512), row(512),
                  pl.BlockSpec((tm, N_BRANCH * BRANCH_W), lambda i: (i, N_BRANCH * D_MODEL // (N_BRANCH * BRANCH_W))),
                  pl.BlockSpec((tm, N_BRANCH * D_MODEL), lambda i: (i, 0)),
                  _resident((N_BRANCH, BRANCH_W, D_MODEL)), _resident((D_MODEL, D_MODEL)),
                  _resident((1, D_MODEL))],
        out_specs=row(D_MODEL),
        out_shape=jax.ShapeDtypeStruct((m, D_MODEL), F32),
        compiler_params=_cparams(("parallel",)),
        name=name,
    )(x, *outs, ub, ub, lw['w_branch'], lw['w_out'], lw['post_g'])


def _page_specs(width, n_pages, pps):
    def spec(j):
        return pl.BlockSpec((1, PAGE, width), lambda b, c, pt: (pt[b * n_pages + c * pps + j], 0, 0))
    return [spec(j) for j in range(pps)]


def _gather_pages(page_refs):
    return jnp.concatenate([r[0] for r in page_refs], axis=0)


def _pool_paged_kernel(pt_ref, *refs):
    pps = len(refs) - 3
    new_ref, o_ref, x_ref = refs[pps], refs[pps + 1], refs[pps + 2]
    means = []
    for r in refs[:pps]:
        means.append(jnp.sum(r[0].reshape(PAGE // NSA_BLOCK, NSA_BLOCK, 128), axis=1) * (1.0 / NSA_BLOCK))
    o_ref[0] = jnp.concatenate(means, axis=0)
    part = jnp.sum(new_ref[...], axis=0, keepdims=True) * (1.0 / NSA_BLOCK)
    rowi = lax.broadcasted_iota(jnp.int32, (128, 128), 0)
    x_ref[0] = jnp.where(rowi == 0, jnp.broadcast_to(part, (128, 128)), 0.0)


def _pool_paged(pt, cache, us, n_seq, n_pages, pps, name):
    nbp = n_pages * (PAGE // NSA_BLOCK)
    per = pps * (PAGE // NSA_BLOCK)
    grid_spec = pltpu.PrefetchScalarGridSpec(
        num_scalar_prefetch=1,
        grid=(n_seq, n_pages // pps),
        in_specs=_page_specs(128, n_pages, pps)
        + [pl.BlockSpec((SAMPLE_ROWS, 128), lambda b, c, pt: (b, _US_CMP // 128))],
        out_specs=[pl.BlockSpec((1, per, 128), lambda b, c, pt: (b, c, 0)),
                   pl.BlockSpec((1, 128, 128), lambda b, c, pt: (b, 0, 0))],
    )
    return pl.pallas_call(
        _pool_paged_kernel,
        grid_spec=grid_spec,
        out_shape=[jax.ShapeDtypeStruct((n_seq, nbp, 128), F32),
                   jax.ShapeDtypeStruct((n_seq, 128, 128), F32)],
        compiler_params=_cparams(("parallel", "arbitrary")),
        name=name,
    )(pt, *([cache] * pps), us)


def _new_row_mask(n_new):
    row = lax.broadcasted_iota(jnp.int32, (SAMPLE_ROWS, SAMPLE_ROWS), 0)
    colk = lax.broadcasted_iota(jnp.int32, (SAMPLE_ROWS, SAMPLE_ROWS), 1)
    return jnp.logical_and(colk <= row, colk < n_new)


def _nsa_paged_kernel(pt_ref, *refs, n_new, cur_blk):
    pps = len(refs) - 16
    pages = refs[:pps]
    (q_ref, misc_ref, oc_ref, cnt_ref, cntx_ref, new_slc_ref, new_win_ref, cwin_ref,
     bl_ref, bn_ref, bw_ref, ex_ref, o_ref, ms, ls, accs) = refs[pps:]
    c = pl.program_id(1)
    nc = pl.num_programs(1)
    r = SAMPLE_ROWS
    hr = NSA_H * r
    qp = _pad_heads_128(q_ref[...], NSA_H, NSA_DH ** -0.5)

    @pl.when(c == 0)
    def _():
        mult = cntx_ref[:, cur_blk:cur_blk + 1].astype(F32)
        kv = new_slc_ref[...].astype(BF16)
        ok = jnp.logical_and(_new_row_mask(n_new), mult > 0.0)
        s = _heads3(_dot_nt(qp, kv), NSA_H) + bn_ref[...]
        s = jnp.where(ok[None], s, NEG).reshape(hr, r)
        w = jnp.broadcast_to(mult[None], (NSA_H, r, r)).reshape(hr, r)
        _flash_init(s, kv, ms, ls, accs, w)

    kv = _gather_pages(pages).astype(BF16)
    w = _dot(cnt_ref[0, 0], ex_ref[...])
    wfull = jnp.broadcast_to(w[None], (NSA_H,) + w.shape).reshape(hr, w.shape[1])
    s3 = _heads3(_dot_nt(qp, kv), NSA_H)

    @pl.when(c < nc - 1)
    def _():
        s = jnp.where((w > 0.0)[None], s3, NEG).reshape(hr, w.shape[1])
        _flash_step(s, kv, ms, ls, accs, wfull)

    @pl.when(c == nc - 1)
    def _():
        s = jnp.where((w > 0.0)[None], s3 + bl_ref[...], NEG).reshape(hr, w.shape[1])
        _flash_step(s, kv, ms, ls, accs, wfull)
        o_s = accs[...] / ls[...]
        kvw = cwin_ref[0].astype(BF16)
        wb = cwin_ref.shape[1]
        row = lax.broadcasted_iota(jnp.int32, (r, wb), 0)
        colk = lax.broadcasted_iota(jnp.int32, (r, wb), 1)
        sw = _heads3(_dot_nt(qp, kvw), NSA_H) + bw_ref[...]
        sw = jnp.where((colk >= row)[None], sw, NEG).reshape(hr, wb)
        kvn = new_win_ref[...].astype(BF16)
        sn = _heads3(_dot_nt(qp, kvn), NSA_H) + bn_ref[...]
        sn = jnp.where(_new_row_mask(n_new)[None], sn, NEG).reshape(hr, r)
        m = jnp.maximum(jnp.max(sw, axis=-1, keepdims=True), jnp.max(sn, axis=-1, keepdims=True))
        pw = jnp.exp(sw - m)
        pn = jnp.exp(sn - m)
        den = jnp.sum(pw, axis=-1, keepdims=True) + jnp.sum(pn, axis=-1, keepdims=True)
        o_w = (_dot(pw.astype(BF16), kvw) + _dot(pn.astype(BF16), kvn)) / den
        g = _sigmoid(misc_ref[:, 0:3 * NSA_H])
        oc = oc_ref[...]
        pieces = []
        for h in range(NSA_H):
            sl = slice(h * r, (h + 1) * r)
            pieces.append(oc[:, h * NSA_DH:(h + 1) * NSA_DH]
                          + g[:, NSA_H + h:NSA_H + h + 1] * o_s[sl, NSA_DH:]
                          + g[:, 2 * NSA_H + h:2 * NSA_H + h + 1] * o_w[sl, NSA_DH:])
        o_ref[...] = jnp.concatenate(pieces, axis=1)


def _nsa_paged(pt, cache, us, oc, cnt4, cntx, cwin, bl, bn, bw, ex, n_seq, n_pages, pps, n_new, name):
    r = SAMPLE_ROWS
    hr = NSA_H * r
    ck = pps * PAGE
    col = lambda off, w: pl.BlockSpec((r, w), lambda b, c, pt: (b, off // w))
    res = lambda shape: pl.BlockSpec(shape, lambda b, c, pt: (0,) * len(shape))
    nbk = cntx.shape[1]
    grid_spec = pltpu.PrefetchScalarGridSpec(
        num_scalar_prefetch=1,
        grid=(n_seq, n_pages // pps),
        in_specs=_page_specs(128, n_pages, pps) + [
            col(_US_NSA_Q, 512), col(_US_MISC, 128),
            pl.BlockSpec((r, 512), lambda b, c, pt: (b, 0)),
            pl.BlockSpec((1, 1, r, cnt4.shape[3]), lambda b, c, pt: (b, c, 0, 0)),
            pl.BlockSpec((r, nbk), lambda b, c, pt: (b, 0)),
            col(_US_SLC, 128), col(_US_WIN, 128),
            pl.BlockSpec((1,) + cwin.shape[1:], lambda b, c, pt: (b, 0, 0)),
            res((NSA_H, r, ck)), res((NSA_H, r, r)), res((NSA_H, r, cwin.shape[1])), res(ex.shape)],
        out_specs=pl.BlockSpec((r, 512), lambda b, c, pt: (b, 0)),
        scratch_shapes=[pltpu.VMEM((hr, 1), F32), pltpu.VMEM((hr, 1), F32), pltpu.VMEM((hr, 128), F32)],
    )
    cur_blk = n_pages * (PAGE // NSA_BLOCK)
    return pl.pallas_call(
        functools.partial(_nsa_paged_kernel, n_new=n_new, cur_blk=cur_blk),
        grid_spec=grid_spec,
        out_shape=jax.ShapeDtypeStruct((n_seq * r, 512), F32),
        compiler_params=_cparams(("parallel", "arbitrary")),
        name=name,
    )(pt, *([cache] * pps), us, us, oc, cnt4, cntx, us, us, cwin, bl, bn, bw, ex)


def _mla_paged_kernel(pt_ref, *refs, n_new):
    pps = len(refs) - 7
    pages = refs[:pps]
    q_ref, new_ref, wuv_ref, o_ref, m_ref, l_ref, acc_ref = refs[pps:]
    c = pl.program_id(1)
    r = SAMPLE_ROWS
    q = q_ref[...].reshape(MLA_H * r, MLA_LAT)

    @pl.when(c == 0)
    def _():
        k = new_ref[...].astype(BF16)
        s = _heads3(_dot_nt(q, k), MLA_H)
        s = jnp.where(_new_row_mask(n_new)[None], s, NEG).reshape(MLA_H * r, r)
        _flash_init(s, k[:, :MLA_KV_RANK], m_ref, l_ref, acc_ref)

    k = _gather_pages(pages).astype(BF16)
    _flash_step(_dot_nt(q, k), k[:, :MLA_KV_RANK], m_ref, l_ref, acc_ref)

    @pl.when(c == pl.num_programs(1) - 1)
    def _():
        _mla_finish(acc_ref[...], l_ref[...], wuv_ref, o_ref, r)


def _mla_paged(pt, cache, q_mla, lat_new, wuv, n_seq, n_pages, pps, n_new, name):
    r = SAMPLE_ROWS
    hr = MLA_H * r
    grid_spec = pltpu.PrefetchScalarGridSpec(
        num_scalar_prefetch=1,
        grid=(n_seq, n_pages // pps),
        in_specs=_page_specs(MLA_LAT, n_pages, pps) + [
            pl.BlockSpec((MLA_H, r, MLA_LAT), lambda b, c, pt: (0, b, 0)),
            pl.BlockSpec((r, MLA_LAT), lambda b, c, pt: (b, 0)),
            pl.BlockSpec((MLA_H, MLA_KV_RANK, MLA_DV), lambda b, c, pt: (0, 0, 0))],
        out_specs=pl.BlockSpec((r, 512), lambda b, c, pt: (b, 0)),
        scratch_shapes=[pltpu.VMEM((hr, 1), F32), pltpu.VMEM((hr, 1), F32),
                        pltpu.VMEM((hr, MLA_KV_RANK), F32)],
    )
    return pl.pallas_call(
        functools.partial(_mla_paged_kernel, n_new=n_new),
        grid_spec=grid_spec,
        out_shape=jax.ShapeDtypeStruct((n_seq * r, 512), F32),
        compiler_params=_cparams(("parallel", "arbitrary")),
        name=name,
    )(pt, *([cache] * pps), q_mla, lat_new, wuv)


def _diff_paged_kernel(pt_ref, *refs, n_new, lam_init):
    pps = len(refs) - 10
    pages = refs[:pps]
    q_ref, new_ref, bl_ref, bn_ref, lp_ref, g_ref, o_ref, m_ref, l_ref, acc_ref = refs[pps:]
    c = pl.program_id(1)
    nc = pl.num_programs(1)
    r = SAMPLE_ROWS
    hr = 2 * DIFF_H * r
    q = _diff_queries(q_ref[...])

    def scores(kv, bias, mask):
        s = _heads3(_dot_nt(q, kv[:, :2 * DIFF_DH]), 2 * DIFF_H)
        if bias is not None:
            s = s + jnp.concatenate([bias, bias], axis=0)
        if mask is not None:
            s = jnp.where(mask[None], s, NEG)
        return s.reshape(hr, kv.shape[0])

    @pl.when(c == 0)
    def _():
        kv = new_ref[...].astype(BF16)
        _flash_init(scores(kv, bn_ref[...], _new_row_mask(n_new)), kv[:, 2 * DIFF_DH:], m_ref, l_ref, acc_ref)

    kv = _gather_pages(pages).astype(BF16)

    @pl.when(c < nc - 1)
    def _():
        _flash_step(scores(kv, None, None), kv[:, 2 * DIFF_DH:], m_ref, l_ref, acc_ref)

    @pl.when(c == nc - 1)
    def _():
        _flash_step(scores(kv, bl_ref[...], None), kv[:, 2 * DIFF_DH:], m_ref, l_ref, acc_ref)
        _diff_finish(acc_ref[...], l_ref[...], lp_ref, g_ref, o_ref, r, lam_init)


def _diff_paged(pt, cache, us, bl, bn, lp, g, lam_init, n_seq, n_pages, pps, n_new, name):
    r = SAMPLE_ROWS
    hr = 2 * DIFF_H * r
    ck = pps * PAGE
    res = lambda shape: pl.BlockSpec(shape, lambda b, c, pt: (0,) * len(shape))
    grid_spec = pltpu.PrefetchScalarGridSpec(
        num_scalar_prefetch=1,
        grid=(n_seq, n_pages // pps),
        in_specs=_page_specs(4 * DIFF_DH, n_pages, pps) + [
            pl.BlockSpec((r, 512), lambda b, c, pt: (b, _US_DIFF_Q // 512)),
            pl.BlockSpec((r, 4 * DIFF_DH), lambda b, c, pt: (b, _US_DIFF_KV // (4 * DIFF_DH))),
            res((DIFF_H, r, ck)), res((DIFF_H, r, r)), res((4, DIFF_DH)), res((1, 2 * DIFF_DH))],
        out_specs=pl.BlockSpec((r, 512), lambda b, c, pt: (b, 0)),
        scratch_shapes=[pltpu.VMEM((hr, 1), F32), pltpu.VMEM((hr, 1), F32),
                        pltpu.VMEM((hr, 2 * DIFF_DH), F32)],
    )
    return pl.pallas_call(
        functools.partial(_diff_paged_kernel, n_new=n_new, lam_init=lam_init),
        grid_spec=grid_spec,
        out_shape=jax.ShapeDtypeStruct((n_seq * r, 512), F32),
        compiler_params=_cparams(("parallel", "arbitrary")),
        name=name,
    )(pt, *([cache] * pps), us, us, bl, bn, lp, g)


def _layer_weights(l, w_in, pre_norm_g, post_norm_g, mla_q_norm_g, w_mla_uq, mla_kv_norm_g, w_mla_uk,
                   w_mla_uv, lam_params, diff_subln_g, mem_norm_g, w_mem_kv, w_branch, w_out):
    w = w_in[l]
    widths = [NSA_H * NSA_DH, 128, 128, 128, 3 * NSA_H, MLA_Q_RANK, MLA_KV_RANK, MLA_ROPE,
              DIFF_H * 2 * DIFF_DH, 4 * DIFF_DH, MEM_H * MEM_DH, N_BRANCH * BRANCH_W, N_BRANCH * D_MODEL]
    offs = np.concatenate([[0], np.cumsum(widths)])
    (nsa_q, cmp_, slc, win, gate, cq, ckv, kr, diff_q, diff_kv, mem_q, z, mg) = [
        w[:, offs[i]:offs[i + 1]] for i in range(len(widths))]
    half = MLA_ROPE // 2
    gate_t = gate.reshape(D_MODEL, NSA_H, 3).transpose(0, 2, 1).reshape(D_MODEL, 3 * NSA_H)
    kr_sw = jnp.concatenate([kr[:, half:], kr[:, :half]], axis=1)
    misc = jnp.concatenate([gate_t, kr, kr_sw, jnp.zeros((D_MODEL, 128 - 3 * NSA_H - 2 * MLA_ROPE), F32)], axis=1)
    w_small = jnp.concatenate([nsa_q, diff_q, mem_q, cq, diff_kv, cmp_, slc, win, ckv, misc], axis=1)
    w_big = jnp.concatenate([mg, z], axis=1)
    uq = w_mla_uq[l].reshape(MLA_Q_RANK, MLA_H, MLA_NOPE + MLA_ROPE)
    uq_r = uq[:, :, MLA_NOPE:]
    uq_rs = jnp.concatenate([uq_r[:, :, half:], uq_r[:, :, :half]], axis=2)
    return {
        'pre_g': pre_norm_g[l], 'post_g': post_norm_g[l].reshape(1, D_MODEL),
        'w_small': w_small.astype(BF16), 'w_big': w_big.astype(BF16),
        'gq': mla_q_norm_g[l].reshape(1, MLA_Q_RANK), 'gkv': mla_kv_norm_g[l].reshape(1, MLA_KV_RANK),
        'w_uq_n': uq[:, :, :MLA_NOPE].reshape(MLA_Q_RANK, MLA_H * MLA_NOPE).astype(BF16),
        'w_uq_r': uq_r.reshape(MLA_Q_RANK, MLA_H * MLA_ROPE).astype(BF16),
        'w_uq_rs': uq_rs.reshape(MLA_Q_RANK, MLA_H * MLA_ROPE).astype(BF16),
        'w_ukT': w_mla_uk[l].reshape(MLA_KV_RANK, MLA_H, MLA_NOPE).transpose(1, 2, 0).astype(BF16),
        'w_uv': w_mla_uv[l].reshape(MLA_KV_RANK, MLA_H, MLA_DV).transpose(1, 0, 2).astype(BF16),
        'lam_params': lam_params[l], 'subln_g': diff_subln_g[l].reshape(1, 2 * DIFF_DH),
        'mem_g': mem_norm_g[l], 'w_mem_kv': w_mem_kv[l].astype(BF16),
        'w_branch': w_branch[l].astype(BF16), 'w_out': w_out[l].astype(BF16),
    }


def _rope_tables(pos):
    half = MLA_ROPE // 2
    inv = ROPE_THETA ** (-jnp.arange(half, dtype=F32) / half)
    ang = pos.astype(F32)[:, None] * inv[None, :]
    cos, sin = jnp.cos(ang), jnp.sin(ang)
    return jnp.concatenate([cos, cos], axis=1), jnp.concatenate([-sin, sin], axis=1)


def _in_proj(x, lw, tm, tag):
    ub = _rms_matmul(x, lw['pre_g'], lw['w_big'], BF16, tm, 2048, tag + '_proj_big')
    us = _rms_matmul(x, lw['pre_g'], lw['w_small'], F32, tm, _US_W, tag + '_proj_small')
    return ub, us


def kernel(x_prompt, x_sample, cache_nsa_cmp_kv, cache_nsa_slc_kv, cache_nsa_win_kv, cache_mla_latent, cache_diff_kv, cache_mem_kv, page_table, mem_prompt, rel_bias, pre_norm_g, post_norm_g, w_in, mla_q_norm_g, w_mla_uq, mla_kv_norm_g, w_mla_uk, w_mla_uv, diff_lambda_q1, diff_lambda_k1, diff_lambda_q2, diff_lambda_k2, diff_subln_g, mem_norm_g, w_mem_kv, w_branch, w_out):
    depth = w_in.shape[0]
    bp, t, _ = x_prompt.shape
    assert bp == 1, "prompt group is a single sequence"
    ns, n_new, _ = x_sample.shape
    n_pages = page_table.shape[1]
    past = n_pages * PAGE
    n_pool = cache_nsa_cmp_kv.shape[1]
    wbuf = cache_nsa_win_kv.shape[2]
    assert wbuf == NSA_WINDOW and n_new <= SAMPLE_ROWS and past % NSA_BLOCK == 0
    tq = min(256, t)
    assert t % tq == 0 and NSA_WINDOW == 2 * tq
    pps = min(PAGES_PER_STEP, n_pages)
    assert n_pages % pps == 0
    ck = pps * PAGE
    r = SAMPLE_ROWS
    nb_p = t // NSA_BLOCK
    nb_s = past // NSA_BLOCK
    nbk_s = nb_s + 128

    lam_params = jnp.stack([diff_lambda_q1, diff_lambda_k1, diff_lambda_q2, diff_lambda_k2], axis=1).astype(F32)

    pos_p = jnp.arange(t, dtype=jnp.int32)
    tok = jnp.arange(r, dtype=jnp.int32)
    pos_s = jnp.tile(past + tok, ns)
    cos_p, sin_p = _rope_tables(pos_p)
    cos_s, sin_s = _rope_tables(pos_s)
    ar = lambda n: jnp.arange(n, dtype=jnp.int32)
    d_tile = jnp.stack([ar(tq)[:, None] - ar(tq)[None, :], tq + ar(tq)[:, None] - ar(tq)[None, :]]).reshape(2 * tq, tq)
    d_cmp_p = pos_p[:, None] - (ar(nb_p)[None, :] * NSA_BLOCK + NSA_BLOCK - 1)
    d_cmp_s = (past + tok)[:, None] - (ar(nbk_s)[None, :] * NSA_BLOCK + NSA_BLOCK - 1)
    d_last = ck + tok[:, None] - ar(ck)[None, :]
    d_new = tok[:, None] - tok[None, :]
    d_win = wbuf + tok[:, None] - ar(wbuf)[None, :]
    bt = _bias_expand(rel_bias, d_tile, 0, BIAS_H, 'bias_tiles').reshape(BIAS_H, 2, tq, tq)
    bias_cmp_p = _bias_expand(rel_bias, d_cmp_p, 0, NSA_H, 'bias_cmp_prompt')
    bias_cmp_s = _bias_expand(rel_bias, d_cmp_s, 0, NSA_H, 'bias_cmp_sample')
    bias_last = _bias_expand(rel_bias, d_last, 0, BIAS_H, 'bias_last_chunk')
    bias_new = _bias_expand(rel_bias, d_new, 0, BIAS_H, 'bias_new_rows')
    bias_win = _bias_expand(rel_bias, d_win, 0, NSA_H, 'bias_window')
    e3 = np.zeros((t // tq, nb_p, tq), np.float32)
    for j in range(t // tq):
        for cidx in range(tq // NSA_BLOCK):
            e3[j, j * (tq // NSA_BLOCK) + cidx, cidx * NSA_BLOCK:(cidx + 1) * NSA_BLOCK] = 1.0
    e3 = jnp.asarray(e3, BF16)
    ex = jnp.asarray(np.repeat(np.eye(ck // NSA_BLOCK, dtype=np.float32), NSA_BLOCK, axis=1), BF16)

    pt_flat = page_table.reshape(-1).astype(jnp.int32)

    xp = x_prompt.reshape(t, D_MODEL)
    xs = jnp.pad(x_sample, ((0, 0), (0, r - n_new), (0, 0))).reshape(ns * r, D_MODEL)
    p_st = [[] for _ in range(6)]
    s_st = [[] for _ in range(5)]
    for l in range(depth):
        lam_init = 0.8 - 0.6 * math.exp(-0.3 * l)
        lw = _layer_weights(l, w_in, pre_norm_g, post_norm_g, mla_q_norm_g, w_mla_uq, mla_kv_norm_g, w_mla_uk,
                            w_mla_uv, lam_params, diff_subln_g, mem_norm_g, w_mem_kv, w_branch, w_out)
        tag = 'l%d' % l
        ub, us = _in_proj(xp, lw, min(512, t), tag + '_p')
        q_mla, lat = _mla_prep(us, cos_p, sin_p, lw, min(512, t), tag + '_p_mla_prep')
        kvc = _pool_prompt(us, tag + '_p_pool')
        oc, cnt = _nsa_cmp(us, kvc, bias_cmp_p, pos_p.reshape(t, 1), nb_p, tq, False, tag + '_p_nsa_cmp')
        cmp_new = us[:, _US_CMP:_US_CMP + 128]
        slc_new = us[:, _US_SLC:_US_SLC + 128]
        win_new = us[:, _US_WIN:_US_WIN + 128]
        dkv_new = us[:, _US_DIFF_KV:_US_DIFF_KV + 4 * DIFF_DH]
        o_nsa = _nsa_prompt(us, oc, cnt, slc_new.astype(BF16), win_new.astype(BF16), bt[:NSA_H], e3, tq,
                            tag + '_p_nsa_attn')
        o_mla = _mla_prompt(q_mla, lat.astype(BF16), lw['w_uv'], tq, tag + '_p_mla_attn')
        o_diff = _diff_prompt(us, dkv_new.astype(BF16), bt[NSA_H:], lw['lam_params'], lw['subln_g'], lam_init, tq,
                              tag + '_p_diff_attn')
        mkv = _rms_matmul(mem_prompt.reshape(N_MEM, D_MODEL), lw['mem_g'], lw['w_mem_kv'], F32,
                          N_MEM, MEM_H * 2 * MEM_DH, tag + '_p_mem_kv')
        o_mem = _mem_attend(us, mkv.reshape(1, N_MEM, -1), tq, False, tag + '_p_mem_attn')
        xp = _merge(xp, [o_nsa, o_mla, o_diff, o_mem], ub, lw, tq, tag + '_p_merge')
        win_keep = min(NSA_WINDOW, t)
        for i, a in enumerate((cmp_new.reshape(1, t, 1, 128), slc_new.reshape(1, t, 1, 128),
                               win_new[t - win_keep:].reshape(1, win_keep, 1, 128), lat.reshape(1, t, MLA_LAT),
                               dkv_new.reshape(1, t, 1, 4 * DIFF_DH), mkv.reshape(1, N_MEM, MEM_H, 2 * MEM_DH))):
            p_st[i].append(a)
        pt_l = pt_flat + l * n_pool
        c_cmp = cache_nsa_cmp_kv.reshape(depth * n_pool, PAGE, 128)
        c_slc = cache_nsa_slc_kv.reshape(depth * n_pool, PAGE, 128)
        c_lat = cache_mla_latent.reshape(depth * n_pool, PAGE, MLA_LAT)
        c_dkv = cache_diff_kv.reshape(depth * n_pool, PAGE, 4 * DIFF_DH)
        c_win = cache_nsa_win_kv[l].reshape(ns, wbuf, 128)
        c_mem = cache_mem_kv[l].reshape(ns, N_MEM, MEM_H * 2 * MEM_DH)
        ub, us = _in_proj(xs, lw, min(512, ns * r), tag + '_s')
        q_mla, lat = _mla_prep(us, cos_s, sin_s, lw, min(512, ns * r), tag + '_s_mla_prep')
        kvc_past, kvc_part = _pool_paged(pt_l, c_cmp, us, ns, n_pages, pps, tag + '_s_pool')
        kvc = jnp.concatenate([kvc_past, kvc_part], axis=1)
        oc, cnt = _nsa_cmp(us, kvc, bias_cmp_s, pos_s.reshape(ns * r, 1), nb_s + 1, r, True, tag + '_s_nsa_cmp')
        cnt4 = cnt[:, :nb_s].reshape(ns, r, n_pages // pps, ck // NSA_BLOCK).transpose(0, 2, 1, 3)
        o_nsa = _nsa_paged(pt_l, c_slc, us, oc, cnt4, cnt, c_win, bias_last[:NSA_H], bias_new[:NSA_H], bias_win, ex,
                           ns, n_pages, pps, n_new, tag + '_s_nsa_attn')
        o_mla = _mla_paged(pt_l, c_lat, q_mla, lat, lw['w_uv'], ns, n_pages, pps, n_new, tag + '_s_mla_attn')
        o_diff = _diff_paged(pt_l, c_dkv, us, bias_last[NSA_H:], bias_new[NSA_H:], lw['lam_params'], lw['subln_g'],
                             lam_init, ns, n_pages, pps, n_new, tag + '_s_diff_attn')
        o_mem = _mem_attend(us, c_mem, r, True, tag + '_s_mem_attn')
        xs = _merge(xs, [o_nsa, o_mla, o_diff, o_mem], ub, lw, min(256, ns * r), tag + '_s_merge')
        us3 = us.reshape(ns, r, _US_W)[:, :n_new]
        new_win = us3[:, :, _US_WIN:_US_WIN + 128]
        s_win = jnp.concatenate([c_win[:, n_new:], new_win], axis=1)
        for i, a in enumerate((us3[:, :, _US_CMP:_US_CMP + 128].reshape(ns, n_new, 1, 128),
                               us3[:, :, _US_SLC:_US_SLC + 128].reshape(ns, n_new, 1, 128),
                               s_win.reshape(ns, wbuf, 1, 128),
                               lat.reshape(ns, r, MLA_LAT)[:, :n_new],
                               us3[:, :, _US_DIFF_KV:_US_DIFF_KV + 4 * DIFF_DH].reshape(ns, n_new, 1, 4 * DIFF_DH))):
            s_st[i].append(a)
    p_nsa_cmp, p_nsa_slc, p_nsa_win, p_mla_latent, p_diff_kv, p_mem_kv = [jnp.stack(a, axis=0) for a in p_st]
    s_nsa_cmp, s_nsa_slc, s_nsa_win, s_mla_latent, s_diff_kv = [jnp.stack(a, axis=0) for a in s_st]
    y_prompt = xp.reshape(1, t, D_MODEL)
    y_sample = xs.reshape(ns, r, D_MODEL)[:, :n_new]
    return (y_prompt, y_sample, p_nsa_cmp, s_nsa_cmp, p_nsa_slc, s_nsa_slc, p_nsa_win, s_nsa_win,
            p_mla_latent, s_mla_latent, p_diff_kv, s_diff_kv, p_mem_kv)
```

```python
import functools
import math

import numpy as np
import jax
import jax.numpy as jnp
from jax import lax
from jax.experimental import pallas as pl
from jax.experimental.pallas import tpu as pltpu

F32 = jnp.float32
BF16 = jnp.bfloat16

D_MODEL = 1024
PAGE = 128
BRANCH_W = 512
N_BRANCH = 4
NSA_H = 8
NSA_DH = 64
NSA_BLOCK = 64
NSA_TOPN = 16
NSA_WINDOW = 512
MLA_H = 8
MLA_Q_RANK = 256
MLA_KV_RANK = 128
MLA_NOPE = 64
MLA_ROPE = 32
MLA_DV = 64
MLA_LAT = MLA_KV_RANK + MLA_ROPE
ROPE_THETA = 10000.0
DIFF_H = 4
DIFF_DH = 64
N_MEM = 256
MEM_H = 4
MEM_DH = 128
N_BUCKETS = 32
MAX_DISTANCE = 128
BIAS_H = NSA_H + DIFF_H
EPS = 1e-6
NEG = -1e30
TINY = float(np.finfo(np.float32).tiny)

VMEM_LIMIT = 56 * 1024 * 1024

SAMPLE_ROWS = 8
PAGES_PER_STEP = 32
TS = 256
QT = 2 * TS
MLA_QW = 256

_US_NSA_Q, _US_DIFF_Q, _US_MEM_Q, _US_CQ, _US_DIFF_KV = 0, 512, 1024, 1536, 1792
_US_CMP, _US_SLC, _US_WIN, _US_CKV, _US_MISC, _US_W = 2048, 2176, 2304, 2432, 2560, 2688
_MISC_KR, _MISC_KRS = 24, 56


def _bucket_thresholds():
    n = np.arange(0, 4 * MAX_DISTANCE, dtype=np.int64)
    exact = N_BUCKETS // 2
    nf = np.maximum(n, exact).astype(np.float32)
    large = exact + (np.log(nf / np.float32(exact)) / np.float32(math.log(MAX_DISTANCE / exact))
                     * np.float32(N_BUCKETS - exact)).astype(np.int32)
    bucket = np.where(n < exact, n, np.minimum(large, N_BUCKETS - 1))
    return [int(np.argmax(bucket >= b)) for b in range(N_BUCKETS)]


_THR = _bucket_thresholds()


def _cparams(sem):
    return pltpu.CompilerParams(dimension_semantics=sem, vmem_limit_bytes=VMEM_LIMIT)


def _resident(shape):
    nd = len(shape)
    return pl.BlockSpec(shape, lambda *a: (0,) * nd, pipeline_mode=pl.Buffered(1))


def _dot(a, b):
    return jnp.dot(a, b, preferred_element_type=F32)


def _dot_nt(a, b):
    return lax.dot_general(a, b, (((1,), (1,)), ((), ())), preferred_element_type=F32)


def _dot_tn(a, b):
    return lax.dot_general(a, b, (((0,), (0,)), ((), ())), preferred_element_type=F32)


def _rms_rows(x, g):
    return x * lax.rsqrt(jnp.mean(x * x, axis=-1, keepdims=True) + EPS) * g


def _rms_matmul_kernel(x_ref, g_ref, w_ref, o_ref):
    y = _rms_rows(x_ref[...], g_ref[...])
    o_ref[...] = _dot(y.astype(BF16), w_ref[...]).astype(o_ref.dtype)


def _rms_matmul(x, g, w, out_dtype, tm, tn, name):
    m, k = x.shape
    n = w.shape[1]
    return pl.pallas_call(
        _rms_matmul_kernel,
        grid=(n // tn, m // tm),
        in_specs=[pl.BlockSpec((tm, k), lambda j, i: (i, 0)),
                  pl.BlockSpec((1, k), lambda j, i: (0, 0)),
                  pl.BlockSpec((k, tn), lambda j, i: (0, j))],
        out_specs=pl.BlockSpec((tm, tn), lambda j, i: (i, j)),
        out_shape=jax.ShapeDtypeStruct((m, n), out_dtype),
        compiler_params=_cparams(("parallel", "parallel")),
        name=name,
    )(x, g.reshape(1, k), w)


def _bias_expand_kernel(tbl_ref, d_ref, o_ref, *, h0):
    h = pl.program_id(0) + h0
    d = d_ref[...]
    out = jnp.full(d.shape, tbl_ref[0, h], F32)
    for b in range(1, N_BUCKETS):
        out = jnp.where(d >= _THR[b], tbl_ref[b, h], out)
    o_ref[0] = out - tbl_ref[N_BUCKETS - 1, h]


def _bias_expand(tbl, dist, h0, nh, name):
    r, c = dist.shape
    tr = min(r, 512)
    return pl.pallas_call(
        functools.partial(_bias_expand_kernel, h0=h0),
        grid=(nh, r // tr),
        in_specs=[pl.BlockSpec(memory_space=pltpu.SMEM),
                  pl.BlockSpec((tr, c), lambda h, i: (i, 0))],
        out_specs=pl.BlockSpec((1, tr, c), lambda h, i: (h, i, 0)),
        out_shape=jax.ShapeDtypeStruct((nh, r, c), F32),
        compiler_params=_cparams(("parallel", "parallel")),
        name=name,
    )(tbl, dist)


def _mla_prep_kernel(cq_ref, ckv_ref, misc_ref, cos_ref, sin_ref, gq_ref, gkv_ref,
                     wn_ref, wr_ref, wrs_ref, wuk_ref, q_ref, lat_ref, *, rope_first):
    scale = (MLA_NOPE + MLA_ROPE) ** -0.5
    cqn = _rms_rows(cq_ref[...], gq_ref[...]).astype(BF16)
    qn = _dot(cqn, wn_ref[...])
    qr = _dot(cqn, wr_ref[...])
    qrs = _dot(cqn, wrs_ref[...])
    cos = cos_ref[...]
    sin = sin_ref[...]
    qw = q_ref.shape[-1]
    lat0, rope0 = (qw - MLA_KV_RANK, 0) if rope_first else (0, MLA_KV_RANK)
    if rope_first:
        q_ref[...] = jnp.zeros(q_ref.shape, BF16)
    for h in range(MLA_H):
        ql = _dot(qn[:, h * MLA_NOPE:(h + 1) * MLA_NOPE].astype(BF16), wuk_ref[h])
        q_ref[h, :, lat0:lat0 + MLA_KV_RANK] = (ql * scale).astype(BF16)
        sl = slice(h * MLA_ROPE, (h + 1) * MLA_ROPE)
        rope = qr[:, sl] * cos + qrs[:, sl] * sin
        q_ref[h, :, rope0:rope0 + MLA_ROPE] = (rope * scale).astype(BF16)
    lat_ref[:, 0:MLA_KV_RANK] = _rms_rows(ckv_ref[...], gkv_ref[...])
    misc = misc_ref[...]
    kr = misc[:, _MISC_KR:_MISC_KR + MLA_ROPE]
    krs = misc[:, _MISC_KRS:_MISC_KRS + MLA_ROPE]
    lat_ref[:, MLA_KV_RANK:MLA_LAT] = kr * cos + krs * sin


def _mla_prep(us, cos, sin, lw, tm, rope_first, name):
    m = us.shape[0]
    qw = MLA_QW if rope_first else MLA_LAT
    col = lambda off, w: pl.BlockSpec((tm, w), lambda i: (i, off // w))
    return pl.pallas_call(
        functools.partial(_mla_prep_kernel, rope_first=rope_first),
        grid=(m // tm,),
        in_specs=[col(_US_CQ, MLA_Q_RANK), col(_US_CKV, MLA_KV_RANK), col(_US_MISC, 128),
                  pl.BlockSpec((tm, MLA_ROPE), lambda i: (i, 0)),
                  pl.BlockSpec((tm, MLA_ROPE), lambda i: (i, 0)),
                  _resident((1, MLA_Q_RANK)), _resident((1, MLA_KV_RANK)),
                  _resident((MLA_Q_RANK, MLA_H * MLA_NOPE)),
                  _resident((MLA_Q_RANK, MLA_H * MLA_ROPE)),
                  _resident((MLA_Q_RANK, MLA_H * MLA_ROPE)),
                  _resident((MLA_H, MLA_NOPE, MLA_KV_RANK))],
        out_specs=[pl.BlockSpec((MLA_H, tm, qw), lambda i: (0, i, 0)),
                   pl.BlockSpec((tm, MLA_LAT), lambda i: (i, 0))],
        out_shape=[jax.ShapeDtypeStruct((MLA_H, m, qw), BF16),
                   jax.ShapeDtypeStruct((m, MLA_LAT), F32)],
        compiler_params=_cparams(("parallel",)),
        name=name,
    )(us, us, us, cos, sin, lw['gq'], lw['gkv'], lw['w_uq_n'], lw['w_uq_r'], lw['w_uq_rs'], lw['w_ukT'])


def _pad_heads_128(q, n_heads, scale):
    r = q.shape[0]
    lane = lax.broadcasted_iota(jnp.int32, (r, 128), 1)
    parts = []
    for h in range(n_heads):
        slab = q[:, (h // 2) * 128:(h // 2 + 1) * 128]
        if h % 2 == 1:
            slab = pltpu.roll(slab, 64, 1)
        parts.append(jnp.where(lane < 64, slab * scale, 0.0).astype(BF16))
    return jnp.concatenate(parts, axis=0)


def _flash_init(s, v, m_ref, l_ref, acc_ref, w=None):
    m = jnp.max(s, axis=-1, keepdims=True)
    p = jnp.exp(s - m)
    if w is not None:
        p = p * w
    m_ref[...] = m
    l_ref[...] = jnp.sum(p, axis=-1, keepdims=True)
    acc_ref[...] = _dot(p.astype(BF16), v)


def _flash_step(s, v, m_ref, l_ref, acc_ref, w=None):
    m_prev = m_ref[...]
    m_new = jnp.maximum(m_prev, jnp.max(s, axis=-1, keepdims=True))
    a = jnp.exp(m_prev - m_new)
    p = jnp.exp(s - m_new)
    if w is not None:
        p = p * w
    l_ref[...] = a * l_ref[...] + jnp.sum(p, axis=-1, keepdims=True)
    acc_ref[...] = a * acc_ref[...] + _dot(p.astype(BF16), v)
    m_ref[...] = m_new


def _heads3(x, h):
    return x.reshape(h, x.shape[0] // h, x.shape[1])


def _sigmoid(x):
    return 1.0 / (1.0 + jnp.exp(-x))


def _nsa_cmp_kernel(q_ref, misc_ref, kvc_ref, bias_ref, qpos_ref, oc_ref, cnt_ref, *, nb_true):
    r = q_ref.shape[0]
    nbk = kvc_ref.shape[-2]
    kv = kvc_ref[...].reshape(nbk, 128).astype(BF16)
    qp = _pad_heads_128(q_ref[...], NSA_H, NSA_DH ** -0.5)
    qpos = qpos_ref[...]
    blk = lax.broadcasted_iota(jnp.int32, (r, nbk), 1)
    valid = jnp.logical_and(blk * NSA_BLOCK + (NSA_BLOCK - 1) <= qpos, blk < nb_true)
    s = _heads3(_dot_nt(qp, kv), NSA_H) + bias_ref[...]
    s = jnp.where(valid[None], s, NEG)
    m = jnp.max(s, axis=-1, keepdims=True)
    e = jnp.where(valid[None], jnp.exp(s - m), 0.0)
    p = e / jnp.maximum(jnp.sum(e, axis=-1, keepdims=True), TINY)
    o = _dot(p.reshape(NSA_H * r, nbk).astype(BF16), kv)
    g0 = _sigmoid(misc_ref[:, 0:NSA_H])
    pieces = [o[h * r:(h + 1) * r, NSA_DH:] * g0[:, h:h + 1] for h in range(NSA_H)]
    oc_ref[...] = jnp.concatenate(pieces, axis=1)
    work = jnp.where(valid, jnp.sum(p, axis=0), -1.0)
    cur = lax.shift_right_arithmetic(qpos, int(math.log2(NSA_BLOCK)))
    cnt = jnp.where(blk == cur, 1.0, 0.0)
    blkf = blk.astype(F32)
    for _ in range(min(NSA_TOPN, nb_true)):
        mx = jnp.max(work, axis=-1, keepdims=True)
        idx = jnp.min(jnp.where(work == mx, blkf, float(nbk)), axis=-1, keepdims=True)
        sel = blkf == idx
        cnt = cnt + jnp.where(jnp.logical_and(sel, mx >= 0.0), 1.0, 0.0)
        work = jnp.where(sel, -2.0, work)
    cnt_ref[...] = cnt.astype(BF16)


def _nsa_cmp(us, kvc, bias_c, qpos, nb_true, rows, per_seq, name):
    m = us.shape[0]
    nbk = kvc.shape[-2]
    col = lambda off, w: pl.BlockSpec((rows, w), lambda i: (i, off // w))
    if per_seq:
        kvc_spec = pl.BlockSpec((1, nbk, 128), lambda i: (i, 0, 0))
        bias_spec = _resident((NSA_H, rows, nbk))
    else:
        kvc_spec = _resident((1, nbk, 128))
        bias_spec = pl.BlockSpec((NSA_H, rows, nbk), lambda i: (0, i, 0))
    return pl.pallas_call(
        functools.partial(_nsa_cmp_kernel, nb_true=nb_true),
        grid=(m // rows,),
        in_specs=[col(_US_NSA_Q, 512), col(_US_MISC, 128), kvc_spec, bias_spec,
                  pl.BlockSpec((rows, 1), lambda i: (i, 0))],
        out_specs=[pl.BlockSpec((rows, 512), lambda i: (i, 0)),
                   pl.BlockSpec((rows, nbk), lambda i: (i, 0))],
        out_shape=[jax.ShapeDtypeStruct((m, 512), F32), jax.ShapeDtypeStruct((m, nbk), BF16)],
        compiler_params=_cparams(("parallel",)),
        name=name,
    )(us, us, kvc, bias_c, qpos)


def _pool_kernel(x_ref, o_ref):
    x = x_ref[...]
    nb = x.shape[0] // NSA_BLOCK
    o_ref[0] = jnp.sum(x.reshape(nb, NSA_BLOCK, 128), axis=1) * (1.0 / NSA_BLOCK)


def _pool_prompt(us, name):
    t = us.shape[0]
    rows = min(t, 2048)
    nb = t // NSA_BLOCK
    return pl.pallas_call(
        _pool_kernel,
        grid=(t // rows,),
        in_specs=[pl.BlockSpec((rows, 128), lambda i: (i, _US_CMP // 128))],
        out_specs=pl.BlockSpec((1, rows // NSA_BLOCK, 128), lambda i: (0, i, 0)),
        out_shape=jax.ShapeDtypeStruct((1, nb, 128), F32),
        compiler_params=_cparams(("parallel",)),
        name=name,
    )(us)


def _rowmax(tiles):
    m = None
    for s in tiles:
        for i in range(s.shape[1] // 128):
            c = s[:, i * 128:(i + 1) * 128]
            m = c if m is None else jnp.maximum(m, c)
    return jnp.max(m, axis=-1, keepdims=True)


def _flash_rows(s_tiles, v, m_ref, acc_ref, idx, w_tiles=None):
    m_prev = m_ref[idx]
    m_new = jnp.maximum(m_prev, _rowmax(s_tiles))
    a = jnp.exp(m_prev - m_new)
    ps = []
    for i, s in enumerate(s_tiles):
        p = jnp.exp(s - m_new)
        if w_tiles is not None:
            p = p * w_tiles[i]
        ps.append(p.astype(BF16))
    p = ps[0] if len(ps) == 1 else jnp.concatenate(ps, axis=1)
    acc_ref[idx] = a * acc_ref[idx] + _dot(p, v)
    m_ref[idx] = m_new


def _causal_tile():
    row = lax.broadcasted_iota(jnp.int32, (TS, TS), 0)
    colk = lax.broadcasted_iota(jnp.int32, (TS, TS), 1)
    return row, colk


def _rows(ref, start, size):
    return ref[pl.ds(pl.multiple_of(start, TS), size), :]


def _mla_prompt_kernel(q_ref, kt_ref, v_ref, wuv_ref, o_ref, m_ref, acc_ref):
    qi = pl.program_id(0)
    m_ref[...] = jnp.full(m_ref.shape, NEG, F32)
    acc_ref[...] = jnp.zeros(acc_ref.shape, F32)

    def qblk(u, h):
        return q_ref[h, u * TS:(u + 1) * TS, :]

    def far(c, carry):
        kts = [kt_ref[2 * c], kt_ref[2 * c + 1]]
        v = _rows(v_ref, c * QT, QT)
        for u in range(2):
            for h in range(MLA_H):
                q = qblk(u, h)
                _flash_rows([_dot(q, kt) for kt in kts], v, m_ref, acc_ref, u * MLA_H + h)
        return carry

    lax.fori_loop(0, qi, far, 0)

    row, colk = _causal_tile()
    causal = colk <= row
    kt_a, kt_b = kt_ref[2 * qi], kt_ref[2 * qi + 1]
    v_ab = _rows(v_ref, qi * QT, QT)
    for h in range(MLA_H):
        s = jnp.where(causal, _dot(qblk(0, h), kt_a), NEG)
        _flash_rows([s], v_ab[:TS], m_ref, acc_ref, h)
    for h in range(MLA_H):
        q = qblk(1, h)
        s1 = jnp.where(causal, _dot(q, kt_b), NEG)
        _flash_rows([_dot(q, kt_a), s1], v_ab, m_ref, acc_ref, MLA_H + h)

    for u in range(2):
        pieces = []
        for h in range(MLA_H):
            acc = acc_ref[u * MLA_H + h]
            o_lat = acc[:, :MLA_KV_RANK] / acc[:, MLA_KV_RANK:]
            pieces.append(_dot(o_lat.astype(BF16), wuv_ref[h]))
        o_ref[u * TS:(u + 1) * TS, :] = jnp.concatenate(pieces, axis=1)


def _mla_prompt(q_mla, kt, v_ext, wuv, name):
    t = v_ext.shape[0]
    nblk = 2 * MLA_H
    return pl.pallas_call(
        _mla_prompt_kernel,
        grid=(t // QT,),
        in_specs=[pl.BlockSpec((MLA_H, QT, MLA_QW), lambda i: (0, i, 0)),
                  _resident(kt.shape), _resident(v_ext.shape), _resident((MLA_H, MLA_KV_RANK, MLA_DV))],
        out_specs=pl.BlockSpec((QT, 512), lambda i: (i, 0)),
        out_shape=jax.ShapeDtypeStruct((t, 512), F32),
        scratch_shapes=[pltpu.VMEM((nblk, TS, 1), F32), pltpu.VMEM((nblk, TS, 2 * MLA_KV_RANK), F32)],
        compiler_params=_cparams(("parallel",)),
        name=name,
    )(q_mla, kt, v_ext, wuv)


def _diff_queries(q):
    r = q.shape[0]
    lane = lax.broadcasted_iota(jnp.int32, (r, 128), 1)
    qs = q * (DIFF_DH ** -0.5)
    m1 = [jnp.where(lane < DIFF_DH, qs[:, h * 128:(h + 1) * 128], 0.0).astype(BF16) for h in range(DIFF_H)]
    m2 = [jnp.where(lane >= DIFF_DH, qs[:, h * 128:(h + 1) * 128], 0.0).astype(BF16) for h in range(DIFF_H)]
    return jnp.concatenate(m1 + m2, axis=0)


def _diff_lambda(lp_ref, lam_init):
    lp = lp_ref[...]
    return (jnp.exp(jnp.sum(lp[0:1] * lp[1:2], axis=-1, keepdims=True))
            - jnp.exp(jnp.sum(lp[2:3] * lp[3:4], axis=-1, keepdims=True)) + lam_init)


def _diff_combine(o1, o2, lam, g, lam_init):
    return _rms_rows(o1 - lam * o2, g) * (1.0 - lam_init)


def _diff_finish(acc, l, lp_ref, g_ref, o_ref, rows, lam_init):
    lam = _diff_lambda(lp_ref, lam_init)
    o = acc / l
    half = DIFF_H * rows
    y = _diff_combine(o[:half], o[half:], lam, g_ref[...], lam_init)
    o_ref[...] = jnp.concatenate([y[h * rows:(h + 1) * rows] for h in range(DIFF_H)], axis=1)


def _stage_queries(qs_ref, make, q_ref, nblk):
    for u in range(2):
        q = make(q_ref[u * TS:(u + 1) * TS, :])
        for r in range(nblk):
            qs_ref[u * nblk + r] = q[r * TS:(r + 1) * TS]


def _diff_prompt_kernel(q_ref, kt_ref, v_ref, bt_ref, lp_ref, g_ref, o_ref, qs_ref, m_ref, acc_ref, *, lam_init):
    qi = pl.program_id(0)
    nblk = 2 * DIFF_H
    m_ref[...] = jnp.full(m_ref.shape, NEG, F32)
    acc_ref[...] = jnp.zeros(acc_ref.shape, F32)
    _stage_queries(qs_ref, _diff_queries, q_ref, nblk)

    def step(c, near_for_a):
        kts = [kt_ref[2 * c], kt_ref[2 * c + 1]]
        v = _rows(v_ref, c * QT, QT)
        for u in range(2):
            for r in range(nblk):
                q = qs_ref[u * nblk + r]
                s0, s1 = _dot(q, kts[0]), _dot(q, kts[1])
                if near_for_a and u == 0:
                    s1 = s1 + bt_ref[r % DIFF_H, 1]
                _flash_rows([s0, s1], v, m_ref, acc_ref, u * nblk + r)

    def far(c, carry):
        step(c, False)
        return carry

    lax.fori_loop(0, jnp.maximum(qi - 1, 0), far, 0)

    @pl.when(qi >= 1)
    def _():
        step(qi - 1, True)

    row, colk = _causal_tile()
    causal = colk <= row
    kt_a, kt_b = kt_ref[2 * qi], kt_ref[2 * qi + 1]
    v_ab = _rows(v_ref, qi * QT, QT)
    for r in range(nblk):
        s = jnp.where(causal, _dot(qs_ref[r], kt_a) + bt_ref[r % DIFF_H, 0], NEG)
        _flash_rows([s], v_ab[:TS], m_ref, acc_ref, r)
    for r in range(nblk):
        q = qs_ref[nblk + r]
        s0 = _dot(q, kt_a) + bt_ref[r % DIFF_H, 1]
        s1 = jnp.where(causal, _dot(q, kt_b) + bt_ref[r % DIFF_H, 0], NEG)
        _flash_rows([s0, s1], v_ab, m_ref, acc_ref, nblk + r)

    lam = _diff_lambda(lp_ref, lam_init)
    g = g_ref[...]
    for u in range(2):
        pieces = []
        for h in range(DIFF_H):
            a1 = acc_ref[u * nblk + h]
            a2 = acc_ref[u * nblk + DIFF_H + h]
            o1 = a1[:, :2 * DIFF_DH] / a1[:, 2 * DIFF_DH:]
            o2 = a2[:, :2 * DIFF_DH] / a2[:, 2 * DIFF_DH:]
            pieces.append(_diff_combine(o1, o2, lam, g, lam_init))
        o_ref[u * TS:(u + 1) * TS, :] = jnp.concatenate(pieces, axis=1)


def _diff_prompt(us, kt, v_ext, bt, lp, g, lam_init, name):
    t = us.shape[0]
    nblk = 2 * 2 * DIFF_H
    return pl.pallas_call(
        functools.partial(_diff_prompt_kernel, lam_init=lam_init),
        grid=(t // QT,),
        in_specs=[pl.BlockSpec((QT, 512), lambda i: (i, _US_DIFF_Q // 512)),
                  _resident(kt.shape), _resident(v_ext.shape), _resident((DIFF_H, 2, TS, TS)),
                  _resident((4, DIFF_DH)), _resident((1, 2 * DIFF_DH))],
        out_specs=pl.BlockSpec((QT, 512), lambda i: (i, 0)),
        out_shape=jax.ShapeDtypeStruct((t, 512), F32),
        scratch_shapes=[pltpu.VMEM((nblk, TS, 128), BF16), pltpu.VMEM((nblk, TS, 1), F32),
                        pltpu.VMEM((nblk, TS, 4 * DIFF_DH), F32)],
        compiler_params=_cparams(("parallel",)),
        name=name,
    )(us, kt, v_ext, bt, lp, g)


def _nsa_prompt_kernel(q_ref, misc_ref, oc_ref, cnt_ref, kt_ref, v_ref, wkt_ref, wv_ref, bt_ref, ex_ref, o_ref,
                       qs_ref, m_ref, acc_ref):
    qi = pl.program_id(0)
    nblk = NSA_H
    m_ref[...] = jnp.full(m_ref.shape, NEG, F32)
    acc_ref[...] = jnp.zeros(acc_ref.shape, F32)
    _stage_queries(qs_ref, lambda q: _pad_heads_128(q, NSA_H, NSA_DH ** -0.5), q_ref, nblk)
    ex = ex_ref[...]

    def mult(u, j):
        return _dot_tn(cnt_ref[j][:, u * TS:(u + 1) * TS], ex)

    def step(c, near_for_a):
        kts = [kt_ref[2 * c], kt_ref[2 * c + 1]]
        v = _rows(v_ref, c * QT, QT)
        for u in range(2):
            w0, w1 = mult(u, 2 * c), mult(u, 2 * c + 1)
            ok0, ok1 = w0 > 0.0, w1 > 0.0
            for h in range(nblk):
                q = qs_ref[u * nblk + h]
                s0, s1 = _dot(q, kts[0]), _dot(q, kts[1])
                if near_for_a and u == 0:
                    s1 = s1 + bt_ref[h, 1]
                _flash_rows([jnp.where(ok0, s0, NEG), jnp.where(ok1, s1, NEG)], v, m_ref, acc_ref,
                            u * nblk + h, [w0, w1])

    def far(c, carry):
        step(c, False)
        return carry

    lax.fori_loop(0, jnp.maximum(qi - 1, 0), far, 0)

    @pl.when(qi >= 1)
    def _():
        step(qi - 1, True)

    row, colk = _causal_tile()
    causal = colk <= row
    kt_a, kt_b = kt_ref[2 * qi], kt_ref[2 * qi + 1]
    v_ab = _rows(v_ref, qi * QT, QT)
    w = mult(0, 2 * qi)
    ok = jnp.logical_and(causal, w > 0.0)
    for h in range(nblk):
        s = jnp.where(ok, _dot(qs_ref[h], kt_a) + bt_ref[h, 0], NEG)
        _flash_rows([s], v_ab[:TS], m_ref, acc_ref, h, [w])
    w0, w1 = mult(1, 2 * qi), mult(1, 2 * qi + 1)
    ok0, ok1 = w0 > 0.0, jnp.logical_and(causal, w1 > 0.0)
    for h in range(nblk):
        q = qs_ref[nblk + h]
        s0 = jnp.where(ok0, _dot(q, kt_a) + bt_ref[h, 1], NEG)
        s1 = jnp.where(ok1, _dot(q, kt_b) + bt_ref[h, 0], NEG)
        _flash_rows([s0, s1], v_ab, m_ref, acc_ref, nblk + h, [w0, w1])

    g = _sigmoid(misc_ref[:, 0:3 * NSA_H])
    oc = oc_ref[...]
    for u in range(2):
        t_abs = 2 * qi + u
        j2, j1 = jnp.maximum(t_abs - 2, 0), jnp.maximum(t_abs - 1, 0)
        mask2 = jnp.logical_and(colk >= row, t_abs >= 2)
        mask1 = jnp.logical_and(colk >= 0, t_abs >= 1)
        kt2, kt1, kt0 = wkt_ref[j2], wkt_ref[j1], wkt_ref[t_abs]
        v2, v1, v0 = _rows(wv_ref, j2 * TS, TS), _rows(wv_ref, j1 * TS, TS), _rows(wv_ref, t_abs * TS, TS)
        rs = slice(u * TS, (u + 1) * TS)
        pieces = []
        for h in range(nblk):
            q = qs_ref[u * nblk + h]
            s2 = jnp.where(mask2, _dot(q, kt2), NEG)
            s1 = jnp.where(mask1, _dot(q, kt1) + bt_ref[h, 1], NEG)
            s0 = jnp.where(causal, _dot(q, kt0) + bt_ref[h, 0], NEG)
            mw = _rowmax([s2, s1, s0])
            p2, p1, p0 = jnp.exp(s2 - mw), jnp.exp(s1 - mw), jnp.exp(s0 - mw)
            lw = jnp.sum(p2 + p1 + p0, axis=-1, keepdims=True)
            ow = (_dot(p2.astype(BF16), v2) + _dot(p1.astype(BF16), v1) + _dot(p0.astype(BF16), v0)) / lw
            acc = acc_ref[u * nblk + h]
            o_s = acc[:, NSA_DH:2 * NSA_DH] / acc[:, 2 * NSA_DH:3 * NSA_DH]
            pieces.append(oc[rs, h * NSA_DH:(h + 1) * NSA_DH]
                          + g[rs, NSA_H + h:NSA_H + h + 1] * o_s
                          + g[rs, 2 * NSA_H + h:2 * NSA_H + h + 1] * ow[:, NSA_DH:])
        o_ref[rs, :] = jnp.concatenate(pieces, axis=1)


def _nsa_prompt(us, oc, cnt8, kt, v_ext, wkt, wv, bt, ex8, name):
    t = us.shape[0]
    nblk = 2 * NSA_H
    col = lambda off, w: pl.BlockSpec((QT, w), lambda i: (i, off // w))
    return pl.pallas_call(
        _nsa_prompt_kernel,
        grid=(t // QT,),
        in_specs=[col(_US_NSA_Q, 512), col(_US_MISC, 128),
                  pl.BlockSpec((QT, 512), lambda i: (i, 0)),
                  pl.BlockSpec((cnt8.shape[0], 8, QT), lambda i: (0, 0, i)),
                  _resident(kt.shape), _resident(v_ext.shape), _resident(wkt.shape), _resident(wv.shape),
                  _resident((NSA_H, 2, TS, TS)), _resident(ex8.shape)],
        out_specs=pl.BlockSpec((QT, 512), lambda i: (i, 0)),
        out_shape=jax.ShapeDtypeStruct((t, 512), F32),
        scratch_shapes=[pltpu.VMEM((nblk, TS, 128), BF16), pltpu.VMEM((nblk, TS, 1), F32),
                        pltpu.VMEM((nblk, TS, 256), F32)],
        compiler_params=_cparams(("parallel",)),
        name=name,
    )(us, us, oc, cnt8, kt, v_ext, wkt, wv, bt, ex8)


def _key_tiles(x):
    t, d = x.shape
    return x.reshape(t // TS, TS, d).transpose(0, 2, 1)


def _with_ones(x, n):
    return jnp.concatenate([x, jnp.ones((x.shape[0], n), x.dtype)], axis=1)


def _mem_kernel(q_ref, mkv_ref, o_ref):
    q = q_ref[...]
    outs = []
    for h in range(MEM_H):
        qh = q[:, h * MEM_DH:(h + 1) * MEM_DH].astype(BF16)
        kvh = mkv_ref[0, :, h * 2 * MEM_DH:(h + 1) * 2 * MEM_DH].astype(BF16)
        k, v = kvh[:, :MEM_DH], kvh[:, MEM_DH:]
        s = _dot_nt(qh, k) * (MEM_DH ** -0.5)
        m = jnp.max(s, axis=-1, keepdims=True)
        e = jnp.exp(s - m)
        p = e / jnp.sum(e, axis=-1, keepdims=True)
        outs.append(_dot(p.astype(BF16), v))
    o_ref[...] = jnp.concatenate(outs, axis=1)


def _mem_attend(us, mkv, rows, per_seq, name):
    m = us.shape[0]
    if per_seq:
        mkv_spec = pl.BlockSpec((1,) + mkv.shape[1:], lambda i: (i, 0, 0))
    else:
        mkv_spec = _resident(mkv.shape)
    return pl.pallas_call(
        _mem_kernel,
        grid=(m // rows,),
        in_specs=[pl.BlockSpec((rows, 512), lambda i: (i, _US_MEM_Q // 512)), mkv_spec],
        out_specs=pl.BlockSpec((rows, 512), lambda i: (i, 0)),
        out_shape=jax.ShapeDtypeStruct((m, 512), F32),
        compiler_params=_cparams(("parallel",)),
        name=name,
    )(us, mkv)


def _merge_kernel(x_ref, o0_ref, o1_ref, o2_ref, o3_ref, z_ref, mg_ref, wb_ref, wo_ref, g_ref, y_ref):
    h = None
    for n, o_ref in enumerate((o0_ref, o1_ref, o2_ref, o3_ref)):
        z = z_ref[:, n * BRANCH_W:(n + 1) * BRANCH_W].astype(F32)
        gated = o_ref[...] * (z * _sigmoid(z))
        br = _dot(gated.astype(BF16), wb_ref[n])
        term = _sigmoid(mg_ref[:, n * D_MODEL:(n + 1) * D_MODEL].astype(F32)) * br
        h = term if h is None else h + term
    y = _dot(h.astype(BF16), wo_ref[...])
    y_ref[...] = x_ref[...] + _rms_rows(y, g_ref[...])


def _merge(x, outs, ub, lw, tm, name):
    m = x.shape[0]
    row = lambda w: pl.BlockSpec((tm, w), lambda i: (i, 0))
    return pl.pallas_call(
        _merge_kernel,
        grid=(m // tm,),
        in_specs=[row(D_MODEL), row(512), row(512), row(512), row(512),
                  pl.BlockSpec((tm, N_BRANCH * BRANCH_W), lambda i: (i, N_BRANCH * D_MODEL // (N_BRANCH * BRANCH_W))),
                  pl.BlockSpec((tm, N_BRANCH * D_MODEL), lambda i: (i, 0)),
                  _resident((N_BRANCH, BRANCH_W, D_MODEL)), _resident((D_MODEL, D_MODEL)),
                  _resident((1, D_MODEL))],
        out_specs=row(D_MODEL),
        out_shape=jax.ShapeDtypeStruct((m, D_MODEL), F32),
        compiler_params=_cparams(("parallel",)),
        name=name,
    )(x, *outs, ub, ub, lw['w_branch'], lw['w_out'], lw['post_g'])


def _page_specs(rows, width, n_pages, pps):
    def spec(j):
        return pl.BlockSpec((1, rows, width), lambda b, c, pt: (pt[b * n_pages + c * pps + j], 0, 0))
    return [spec(j) for j in range(pps)]


def _gather_pages(page_refs):
    return jnp.concatenate([r[0] for r in page_refs], axis=0)


def _pool_paged_kernel(pt_ref, *refs):
    pps = len(refs) - 3
    new_ref, o_ref, x_ref = refs[pps], refs[pps + 1], refs[pps + 2]
    means = []
    for r in refs[:pps]:
        means.append(jnp.sum(r[0].reshape(PAGE // NSA_BLOCK, NSA_BLOCK, 128), axis=1) * (1.0 / NSA_BLOCK))
    o_ref[0] = jnp.concatenate(means, axis=0)
    part = jnp.sum(new_ref[...], axis=0, keepdims=True) * (1.0 / NSA_BLOCK)
    rowi = lax.broadcasted_iota(jnp.int32, (128, 128), 0)
    x_ref[0] = jnp.where(rowi == 0, jnp.broadcast_to(part, (128, 128)), 0.0)


def _pool_paged(pt, cache, us, n_seq, n_pages, pps, name):
    nbp = n_pages * (PAGE // NSA_BLOCK)
    per = pps * (PAGE // NSA_BLOCK)
    grid_spec = pltpu.PrefetchScalarGridSpec(
        num_scalar_prefetch=1,
        grid=(n_seq, n_pages // pps),
        in_specs=_page_specs(PAGE, 128, n_pages, pps)
        + [pl.BlockSpec((SAMPLE_ROWS, 128), lambda b, c, pt: (b, _US_CMP // 128))],
        out_specs=[pl.BlockSpec((1, per, 128), lambda b, c, pt: (b, c, 0)),
                   pl.BlockSpec((1, 128, 128), lambda b, c, pt: (b, 0, 0))],
    )
    return pl.pallas_call(
        _pool_paged_kernel,
        grid_spec=grid_spec,
        out_shape=[jax.ShapeDtypeStruct((n_seq, nbp, 128), F32),
                   jax.ShapeDtypeStruct((n_seq, 128, 128), F32)],
        compiler_params=_cparams(("parallel", "arbitrary")),
        name=name,
    )(pt, *([cache] * pps), us)


def _new_row_mask(n_new):
    row = lax.broadcasted_iota(jnp.int32, (SAMPLE_ROWS, SAMPLE_ROWS), 0)
    colk = lax.broadcasted_iota(jnp.int32, (SAMPLE_ROWS, SAMPLE_ROWS), 1)
    return jnp.logical_and(colk <= row, colk < n_new)


def _nsa_paged_kernel(pt_ref, *refs, n_new, cur_blk):
    pps = len(refs) - 16
    pages = refs[:pps]
    (q_ref, misc_ref, oc_ref, cnt_ref, cntx_ref, new_slc_ref, new_win_ref, cwin_ref,
     bl_ref, bn_ref, bw_ref, ex_ref, o_ref, ms, ls, accs) = refs[pps:]
    c = pl.program_id(1)
    nc = pl.num_programs(1)
    r = SAMPLE_ROWS
    hr = NSA_H * r
    qp = _pad_heads_128(q_ref[...], NSA_H, NSA_DH ** -0.5)

    @pl.when(c == 0)
    def _():
        mult = cntx_ref[:, cur_blk:cur_blk + 1].astype(F32)
        kv = new_slc_ref[...].astype(BF16)
        ok = jnp.logical_and(_new_row_mask(n_new), mult > 0.0)
        s = _heads3(_dot_nt(qp, kv), NSA_H) + bn_ref[...]
        s = jnp.where(ok[None], s, NEG).reshape(hr, r)
        w = jnp.broadcast_to(mult[None], (NSA_H, r, r)).reshape(hr, r)
        _flash_init(s, kv, ms, ls, accs, w)

    kv = _gather_pages(pages).astype(BF16)
    w = _dot(cnt_ref[0, 0], ex_ref[...])
    wfull = jnp.broadcast_to(w[None], (NSA_H,) + w.shape).reshape(hr, w.shape[1])
    s3 = _heads3(_dot_nt(qp, kv), NSA_H)

    @pl.when(c < nc - 1)
    def _():
        s = jnp.where((w > 0.0)[None], s3, NEG).reshape(hr, w.shape[1])
        _flash_step(s, kv, ms, ls, accs, wfull)

    @pl.when(c == nc - 1)
    def _():
        s = jnp.where((w > 0.0)[None], s3 + bl_ref[...], NEG).reshape(hr, w.shape[1])
        _flash_step(s, kv, ms, ls, accs, wfull)
        o_s = accs[...] / ls[...]
        kvw = cwin_ref[0].astype(BF16)
        wb = cwin_ref.shape[1]
        row = lax.broadcasted_iota(jnp.int32, (r, wb), 0)
        colk = lax.broadcasted_iota(jnp.int32, (r, wb), 1)
        sw = _heads3(_dot_nt(qp, kvw), NSA_H) + bw_ref[...]
        sw = jnp.where((colk >= row)[None], sw, NEG).reshape(hr, wb)
        kvn = new_win_ref[...].astype(BF16)
        sn = _heads3(_dot_nt(qp, kvn), NSA_H) + bn_ref[...]
        sn = jnp.where(_new_row_mask(n_new)[None], sn, NEG).reshape(hr, r)
        m = jnp.maximum(jnp.max(sw, axis=-1, keepdims=True), jnp.max(sn, axis=-1, keepdims=True))
        pw = jnp.exp(sw - m)
        pn = jnp.exp(sn - m)
        den = jnp.sum(pw, axis=-1, keepdims=True) + jnp.sum(pn, axis=-1, keepdims=True)
        o_w = (_dot(pw.astype(BF16), kvw) + _dot(pn.astype(BF16), kvn)) / den
        g = _sigmoid(misc_ref[:, 0:3 * NSA_H])
        oc = oc_ref[...]
        pieces = []
        for h in range(NSA_H):
            sl = slice(h * r, (h + 1) * r)
            pieces.append(oc[:, h * NSA_DH:(h + 1) * NSA_DH]
                          + g[:, NSA_H + h:NSA_H + h + 1] * o_s[sl, NSA_DH:]
                          + g[:, 2 * NSA_H + h:2 * NSA_H + h + 1] * o_w[sl, NSA_DH:])
        o_ref[...] = jnp.concatenate(pieces, axis=1)


def _nsa_paged(pt, cache, us, oc, cnt4, cntx, cwin, seq0, bl, bn, bw, ex, n_seq, n_pages, pps, n_new, name):
    r = SAMPLE_ROWS
    hr = NSA_H * r
    ck = pps * PAGE
    col = lambda off, w: pl.BlockSpec((r, w), lambda b, c, pt: (b, off // w))
    res = lambda shape: pl.BlockSpec(shape, lambda b, c, pt: (0,) * len(shape))
    nbk = cntx.shape[1]
    grid_spec = pltpu.PrefetchScalarGridSpec(
        num_scalar_prefetch=1,
        grid=(n_seq, n_pages // pps),
        in_specs=_page_specs(PAGE, 128, n_pages, pps) + [
            col(_US_NSA_Q, 512), col(_US_MISC, 128),
            pl.BlockSpec((r, 512), lambda b, c, pt: (b, 0)),
            pl.BlockSpec((1, 1, r, cnt4.shape[3]), lambda b, c, pt: (b, c, 0, 0)),
            pl.BlockSpec((r, nbk), lambda b, c, pt: (b, 0)),
            col(_US_SLC, 128), col(_US_WIN, 128),
            pl.BlockSpec((1,) + cwin.shape[1:], lambda b, c, pt: (b + seq0, 0, 0)),
            res((NSA_H, r, ck)), res((NSA_H, r, r)), res((NSA_H, r, cwin.shape[1])), res(ex.shape)],
        out_specs=pl.BlockSpec((r, 512), lambda b, c, pt: (b, 0)),
        scratch_shapes=[pltpu.VMEM((hr, 1), F32), pltpu.VMEM((hr, 1), F32), pltpu.VMEM((hr, 128), F32)],
    )
    cur_blk = n_pages * (PAGE // NSA_BLOCK)
    return pl.pallas_call(
        functools.partial(_nsa_paged_kernel, n_new=n_new, cur_blk=cur_blk),
        grid_spec=grid_spec,
        out_shape=jax.ShapeDtypeStruct((n_seq * r, 512), F32),
        compiler_params=_cparams(("parallel", "arbitrary")),
        name=name,
    )(pt, *([cache] * pps), us, us, oc, cnt4, cntx, us, us, cwin, bl, bn, bw, ex)


def _mla_finish(acc, l, wuv_ref, o_ref, rows):
    o_lat = acc / l
    pieces = [_dot(o_lat[h * rows:(h + 1) * rows].astype(BF16), wuv_ref[h]) for h in range(MLA_H)]
    o_ref[...] = jnp.concatenate(pieces, axis=1)


def _mla_paged_kernel(pt_ref, *refs, n_new):
    pps = len(refs) - 7
    pages = refs[:pps]
    q_ref, new_ref, wuv_ref, o_ref, m_ref, l_ref, acc_ref = refs[pps:]
    c = pl.program_id(1)
    r = SAMPLE_ROWS
    q = q_ref[...].reshape(MLA_H * r, MLA_LAT)

    @pl.when(c == 0)
    def _():
        k = new_ref[...].astype(BF16)
        s = _heads3(_dot_nt(q, k), MLA_H)
        s = jnp.where(_new_row_mask(n_new)[None], s, NEG).reshape(MLA_H * r, r)
        _flash_init(s, k[:, :MLA_KV_RANK], m_ref, l_ref, acc_ref)

    k = _gather_pages(pages).astype(BF16)
    _flash_step(_dot_nt(q, k), k[:, :MLA_KV_RANK], m_ref, l_ref, acc_ref)

    @pl.when(c == pl.num_programs(1) - 1)
    def _():
        _mla_finish(acc_ref[...], l_ref[...], wuv_ref, o_ref, r)


def _mla_paged(pt, cache, q_mla, lat_new, wuv, n_seq, n_pages, pps, n_new, name):
    r = SAMPLE_ROWS
    hr = MLA_H * r
    grid_spec = pltpu.PrefetchScalarGridSpec(
        num_scalar_prefetch=1,
        grid=(n_seq, n_pages // pps),
        in_specs=_page_specs(PAGE, MLA_LAT, n_pages, pps) + [
            pl.BlockSpec((MLA_H, r, MLA_LAT), lambda b, c, pt: (0, b, 0)),
            pl.BlockSpec((r, MLA_LAT), lambda b, c, pt: (b, 0)),
            pl.BlockSpec((MLA_H, MLA_KV_RANK, MLA_DV), lambda b, c, pt: (0, 0, 0))],
        out_specs=pl.BlockSpec((r, 512), lambda b, c, pt: (b, 0)),
        scratch_shapes=[pltpu.VMEM((hr, 1), F32), pltpu.VMEM((hr, 1), F32),
                        pltpu.VMEM((hr, MLA_KV_RANK), F32)],
    )
    return pl.pallas_call(
        functools.partial(_mla_paged_kernel, n_new=n_new),
        grid_spec=grid_spec,
        out_shape=jax.ShapeDtypeStruct((n_seq * r, 512), F32),
        compiler_params=_cparams(("parallel", "arbitrary")),
        name=name,
    )(pt, *([cache] * pps), q_mla, lat_new, wuv)


def _diff_paged_kernel(pt_ref, *refs, n_new, lam_init):
    pps = len(refs) - 10
    pages = refs[:pps]
    q_ref, new_ref, bl_ref, bn_ref, lp_ref, g_ref, o_ref, m_ref, l_ref, acc_ref = refs[pps:]
    c = pl.program_id(1)
    nc = pl.num_programs(1)
    r = SAMPLE_ROWS
    hr = 2 * DIFF_H * r
    q = _diff_queries(q_ref[...])

    def scores(k, bias, mask):
        s = _heads3(_dot_nt(q, k), 2 * DIFF_H)
        if bias is not None:
            s = s + jnp.concatenate([bias, bias], axis=0)
        if mask is not None:
            s = jnp.where(mask[None], s, NEG)
        return s.reshape(hr, k.shape[0])

    @pl.when(c == 0)
    def _():
        kv = new_ref[...].astype(BF16)
        _flash_init(scores(kv[:, :2 * DIFF_DH], bn_ref[...], _new_row_mask(n_new)), kv[:, 2 * DIFF_DH:],
                    m_ref, l_ref, acc_ref)

    k = jnp.concatenate([p[0, pl.ds(0, PAGE, stride=2), :] for p in pages], axis=0).astype(BF16)
    v = jnp.concatenate([p[0, pl.ds(1, PAGE, stride=2), :] for p in pages], axis=0).astype(BF16)

    @pl.when(c < nc - 1)
    def _():
        _flash_step(scores(k, None, None), v, m_ref, l_ref, acc_ref)

    @pl.when(c == nc - 1)
    def _():
        _flash_step(scores(k, bl_ref[...], None), v, m_ref, l_ref, acc_ref)
        _diff_finish(acc_ref[...], l_ref[...], lp_ref, g_ref, o_ref, r, lam_init)


def _diff_paged(pt, cache, us, bl, bn, lp, g, lam_init, n_seq, n_pages, pps, n_new, name):
    r = SAMPLE_ROWS
    hr = 2 * DIFF_H * r
    ck = pps * PAGE
    res = lambda shape: pl.BlockSpec(shape, lambda b, c, pt: (0,) * len(shape))
    grid_spec = pltpu.PrefetchScalarGridSpec(
        num_scalar_prefetch=1,
        grid=(n_seq, n_pages // pps),
        in_specs=_page_specs(2 * PAGE, 2 * DIFF_DH, n_pages, pps) + [
            pl.BlockSpec((r, 512), lambda b, c, pt: (b, _US_DIFF_Q // 512)),
            pl.BlockSpec((r, 4 * DIFF_DH), lambda b, c, pt: (b, _US_DIFF_KV // (4 * DIFF_DH))),
            res((DIFF_H, r, ck)), res((DIFF_H, r, r)), res((4, DIFF_DH)), res((1, 2 * DIFF_DH))],
        out_specs=pl.BlockSpec((r, 512), lambda b, c, pt: (b, 0)),
        scratch_shapes=[pltpu.VMEM((hr, 1), F32), pltpu.VMEM((hr, 1), F32),
                        pltpu.VMEM((hr, 2 * DIFF_DH), F32)],
    )
    return pl.pallas_call(
        functools.partial(_diff_paged_kernel, n_new=n_new, lam_init=lam_init),
        grid_spec=grid_spec,
        out_shape=jax.ShapeDtypeStruct((n_seq * r, 512), F32),
        compiler_params=_cparams(("parallel", "arbitrary")),
        name=name,
    )(pt, *([cache] * pps), us, us, bl, bn, lp, g)


def _layer_weights(l, w_in, pre_norm_g, post_norm_g, mla_q_norm_g, w_mla_uq, mla_kv_norm_g, w_mla_uk,
                   w_mla_uv, lam_params, diff_subln_g, mem_norm_g, w_mem_kv, w_branch, w_out):
    w = w_in[l]
    widths = [NSA_H * NSA_DH, 128, 128, 128, 3 * NSA_H, MLA_Q_RANK, MLA_KV_RANK, MLA_ROPE,
              DIFF_H * 2 * DIFF_DH, 4 * DIFF_DH, MEM_H * MEM_DH, N_BRANCH * BRANCH_W, N_BRANCH * D_MODEL]
    offs = np.concatenate([[0], np.cumsum(widths)])
    (nsa_q, cmp_, slc, win, gate, cq, ckv, kr, diff_q, diff_kv, mem_q, z, mg) = [
        w[:, offs[i]:offs[i + 1]] for i in range(len(widths))]
    half = MLA_ROPE // 2
    gate_t = gate.reshape(D_MODEL, NSA_H, 3).transpose(0, 2, 1).reshape(D_MODEL, 3 * NSA_H)
    kr_sw = jnp.concatenate([kr[:, half:], kr[:, :half]], axis=1)
    misc = jnp.concatenate([gate_t, kr, kr_sw, jnp.zeros((D_MODEL, 128 - 3 * NSA_H - 2 * MLA_ROPE), F32)], axis=1)
    w_small = jnp.concatenate([nsa_q, diff_q, mem_q, cq, diff_kv, cmp_, slc, win, ckv, misc], axis=1)
    w_big = jnp.concatenate([mg, z], axis=1)
    uq = w_mla_uq[l].reshape(MLA_Q_RANK, MLA_H, MLA_NOPE + MLA_ROPE)
    uq_r = uq[:, :, MLA_NOPE:]
    uq_rs = jnp.concatenate([uq_r[:, :, half:], uq_r[:, :, :half]], axis=2)
    return {
        'pre_g': pre_norm_g[l], 'post_g': post_norm_g[l].reshape(1, D_MODEL),
        'w_small': w_small.astype(BF16), 'w_big': w_big.astype(BF16),
        'gq': mla_q_norm_g[l].reshape(1, MLA_Q_RANK), 'gkv': mla_kv_norm_g[l].reshape(1, MLA_KV_RANK),
        'w_uq_n': uq[:, :, :MLA_NOPE].reshape(MLA_Q_RANK, MLA_H * MLA_NOPE).astype(BF16),
        'w_uq_r': uq_r.reshape(MLA_Q_RANK, MLA_H * MLA_ROPE).astype(BF16),
        'w_uq_rs': uq_rs.reshape(MLA_Q_RANK, MLA_H * MLA_ROPE).astype(BF16),
        'w_ukT': w_mla_uk[l].reshape(MLA_KV_RANK, MLA_H, MLA_NOPE).transpose(1, 2, 0).astype(BF16),
        'w_uv': w_mla_uv[l].reshape(MLA_KV_RANK, MLA_H, MLA_DV).transpose(1, 0, 2).astype(BF16),
        'lam_params': lam_params[l], 'subln_g': diff_subln_g[l].reshape(1, 2 * DIFF_DH),
        'mem_g': mem_norm_g[l], 'w_mem_kv': w_mem_kv[l].astype(BF16),
        'w_branch': w_branch[l].astype(BF16), 'w_out': w_out[l].astype(BF16),
    }


def _rope_tables(pos):
    half = MLA_ROPE // 2
    inv = ROPE_THETA ** (-jnp.arange(half, dtype=F32) / half)
    ang = pos.astype(F32)[:, None] * inv[None, :]
    cos, sin = jnp.cos(ang), jnp.sin(ang)
    return jnp.concatenate([cos, cos], axis=1), jnp.concatenate([-sin, sin], axis=1)


def _in_proj(x, lw, tm, tag):
    ub = _rms_matmul(x, lw['pre_g'], lw['w_big'], BF16, tm, 2048, tag + '_proj_big')
    us = _rms_matmul(x, lw['pre_g'], lw['w_small'], F32, tm, _US_W, tag + '_proj_small')
    return ub, us


def kernel(x_prompt, x_sample, cache_nsa_cmp_kv, cache_nsa_slc_kv, cache_nsa_win_kv, cache_mla_latent, cache_diff_kv, cache_mem_kv, page_table, mem_prompt, rel_bias, pre_norm_g, post_norm_g, w_in, mla_q_norm_g, w_mla_uq, mla_kv_norm_g, w_mla_uk, w_mla_uv, diff_lambda_q1, diff_lambda_k1, diff_lambda_q2, diff_lambda_k2, diff_subln_g, mem_norm_g, w_mem_kv, w_branch, w_out):
    depth = w_in.shape[0]
    bp, t, _ = x_prompt.shape
    assert bp == 1, "prompt group is a single sequence"
    ns, n_new, _ = x_sample.shape
    n_pages = page_table.shape[1]
    past = n_pages * PAGE
    n_pool = cache_nsa_cmp_kv.shape[1]
    wbuf = cache_nsa_win_kv.shape[2]
    assert wbuf == NSA_WINDOW and n_new <= SAMPLE_ROWS and past % NSA_BLOCK == 0
    assert t % QT == 0 and NSA_WINDOW == 2 * TS
    pps = min(PAGES_PER_STEP, n_pages)
    assert n_pages % pps == 0
    ck = pps * PAGE
    r = SAMPLE_ROWS
    nb_p = t // NSA_BLOCK
    nb_s = past // NSA_BLOCK
    nbk_s = nb_s + 128
    bpt = TS // NSA_BLOCK

    lam_params = jnp.stack([diff_lambda_q1, diff_lambda_k1, diff_lambda_q2, diff_lambda_k2], axis=1).astype(F32)

    pos_p = jnp.arange(t, dtype=jnp.int32)
    tok = jnp.arange(r, dtype=jnp.int32)
    pos_s = jnp.tile(past + tok, ns)
    cos_p, sin_p = _rope_tables(pos_p)
    cos_s, sin_s = _rope_tables(pos_s)
    ar = lambda n: jnp.arange(n, dtype=jnp.int32)
    d_tile = jnp.stack([ar(TS)[:, None] - ar(TS)[None, :], TS + ar(TS)[:, None] - ar(TS)[None, :]]).reshape(2 * TS, TS)
    d_cmp_p = pos_p[:, None] - (ar(nb_p)[None, :] * NSA_BLOCK + NSA_BLOCK - 1)
    d_cmp_s = (past + tok)[:, None] - (ar(nbk_s)[None, :] * NSA_BLOCK + NSA_BLOCK - 1)
    d_last = ck + tok[:, None] - ar(ck)[None, :]
    d_new = tok[:, None] - tok[None, :]
    d_win = wbuf + tok[:, None] - ar(wbuf)[None, :]
    bt = _bias_expand(rel_bias, d_tile, 0, BIAS_H, 'bias_tiles').reshape(BIAS_H, 2, TS, TS)
    bias_cmp_p = _bias_expand(rel_bias, d_cmp_p, 0, NSA_H, 'bias_cmp_prompt')
    bias_cmp_s = _bias_expand(rel_bias, d_cmp_s, 0, NSA_H, 'bias_cmp_sample')
    bias_last = _bias_expand(rel_bias, d_last, 0, BIAS_H, 'bias_last_chunk')
    bias_new = _bias_expand(rel_bias, d_new, 0, BIAS_H, 'bias_new_rows')
    bias_win = _bias_expand(rel_bias, d_win, 0, NSA_H, 'bias_window')
    ex8 = np.zeros((8, TS), np.float32)
    for cidx in range(bpt):
        ex8[cidx, cidx * NSA_BLOCK:(cidx + 1) * NSA_BLOCK] = 1.0
    ex8 = jnp.asarray(ex8, BF16)
    ex = jnp.asarray(np.repeat(np.eye(ck // NSA_BLOCK, dtype=np.float32), NSA_BLOCK, axis=1), BF16)

    pt_flat = page_table.reshape(-1).astype(jnp.int32)
    c_cmp = cache_nsa_cmp_kv.reshape(depth * n_pool, PAGE, 128)
    c_slc = cache_nsa_slc_kv.reshape(depth * n_pool, PAGE, 128)
    c_lat = cache_mla_latent.reshape(depth * n_pool, PAGE, MLA_LAT)
    c_dkv = cache_diff_kv.reshape(depth * n_pool, 2 * PAGE, 2 * DIFF_DH)
    c_win = cache_nsa_win_kv.reshape(depth * ns, wbuf, 128)

    xp = x_prompt.reshape(t, D_MODEL)
    xs = jnp.pad(x_sample, ((0, 0), (0, r - n_new), (0, 0))).reshape(ns * r, D_MODEL)
    p_st = [[] for _ in range(6)]
    s_st = [[] for _ in range(5)]
    for l in range(depth):
        lam_init = 0.8 - 0.6 * math.exp(-0.3 * l)
        lw = _layer_weights(l, w_in, pre_norm_g, post_norm_g, mla_q_norm_g, w_mla_uq, mla_kv_norm_g, w_mla_uk,
                            w_mla_uv, lam_params, diff_subln_g, mem_norm_g, w_mem_kv, w_branch, w_out)
        tag = 'l%d' % l
        ub, us = _in_proj(xp, lw, min(512, t), tag + '_p')
        q_mla, lat = _mla_prep(us, cos_p, sin_p, lw, min(512, t), True, tag + '_p_mla_prep')
        kvc = _pool_prompt(us, tag + '_p_pool')
        oc, cnt = _nsa_cmp(us, kvc, bias_cmp_p, pos_p.reshape(t, 1), nb_p, TS, False, tag + '_p_nsa_cmp')
        cmp_new = us[:, _US_CMP:_US_CMP + 128]
        slc_new = us[:, _US_SLC:_US_SLC + 128]
        win_new = us[:, _US_WIN:_US_WIN + 128]
        dkv_new = us[:, _US_DIFF_KV:_US_DIFF_KV + 4 * DIFF_DH]
        cnt8 = jnp.pad(cnt.reshape(t, nb_p // bpt, bpt).transpose(1, 2, 0), ((0, 0), (0, 8 - bpt), (0, 0)))
        slc_bf, win_bf = slc_new.astype(BF16), win_new.astype(BF16)
        o_nsa = _nsa_prompt(us, oc, cnt8, _key_tiles(slc_bf), _with_ones(slc_bf, 128), _key_tiles(win_bf), win_bf,
                            bt[:NSA_H], ex8, tag + '_p_nsa_attn')
        lat_bf = lat.astype(BF16)
        mla_k = jnp.concatenate([lat_bf[:, MLA_KV_RANK:], jnp.zeros((t, MLA_QW - MLA_LAT), BF16),
                                 lat_bf[:, :MLA_KV_RANK]], axis=1)
        o_mla = _mla_prompt(q_mla, _key_tiles(mla_k), _with_ones(lat_bf[:, :MLA_KV_RANK], MLA_KV_RANK), lw['w_uv'],
                            tag + '_p_mla_attn')
        dkv_bf = dkv_new.astype(BF16)
        o_diff = _diff_prompt(us, _key_tiles(dkv_bf[:, :2 * DIFF_DH]), _with_ones(dkv_bf[:, 2 * DIFF_DH:], 2 * DIFF_DH),
                              bt[NSA_H:], lw['lam_params'], lw['subln_g'], lam_init, tag + '_p_diff_attn')
        mkv = _rms_matmul(mem_prompt.reshape(N_MEM, D_MODEL), lw['mem_g'], lw['w_mem_kv'], F32,
                          N_MEM, MEM_H * 2 * MEM_DH, tag + '_p_mem_kv')
        o_mem = _mem_attend(us, mkv.reshape(1, N_MEM, -1), TS, False, tag + '_p_mem_attn')
        xp = _merge(xp, [o_nsa, o_mla, o_diff, o_mem], ub, lw, TS, tag + '_p_merge')
        win_keep = min(NSA_WINDOW, t)
        for i, a in enumerate((cmp_new.reshape(1, t, 1, 128), slc_new.reshape(1, t, 1, 128),
                               win_new[t - win_keep:].reshape(1, win_keep, 1, 128), lat.reshape(1, t, MLA_LAT),
                               dkv_new.reshape(1, t, 1, 4 * DIFF_DH), mkv.reshape(1, N_MEM, MEM_H, 2 * MEM_DH))):
            p_st[i].append(a)
        pt_l = pt_flat + l * n_pool
        ub, us = _in_proj(xs, lw, min(512, ns * r), tag + '_s')
        q_mla, lat = _mla_prep(us, cos_s, sin_s, lw, min(512, ns * r), False, tag + '_s_mla_prep')
        kvc_past, kvc_part = _pool_paged(pt_l, c_cmp, us, ns, n_pages, pps, tag + '_s_pool')
        kvc = jnp.concatenate([kvc_past, kvc_part], axis=1)
        oc, cnt = _nsa_cmp(us, kvc, bias_cmp_s, pos_s.reshape(ns * r, 1), nb_s + 1, r, True, tag + '_s_nsa_cmp')
        cnt4 = cnt[:, :nb_s].reshape(ns, r, n_pages // pps, ck // NSA_BLOCK).transpose(0, 2, 1, 3)
        o_nsa = _nsa_paged(pt_l, c_slc, us, oc, cnt4, cnt, c_win, l * ns, bias_last[:NSA_H], bias_new[:NSA_H],
                           bias_win, ex, ns, n_pages, pps, n_new, tag + '_s_nsa_attn')
        o_mla = _mla_paged(pt_l, c_lat, q_mla, lat, lw['w_uv'], ns, n_pages, pps, n_new, tag + '_s_mla_attn')
        o_diff = _diff_paged(pt_l, c_dkv, us, bias_last[NSA_H:], bias_new[NSA_H:], lw['lam_params'], lw['subln_g'],
                             lam_init, ns, n_pages, pps, n_new, tag + '_s_diff_attn')
        c_mem = cache_mem_kv[l].reshape(ns, N_MEM, MEM_H * 2 * MEM_DH)
        o_mem = _mem_attend(us, c_mem, r, True, tag + '_s_mem_attn')
        xs = _merge(xs, [o_nsa, o_mla, o_diff, o_mem], ub, lw, min(256, ns * r), tag + '_s_merge')
        us3 = us.reshape(ns, r, _US_W)[:, :n_new]
        new_win = us3[:, :, _US_WIN:_US_WIN + 128]
        s_win = jnp.concatenate([c_win[l * ns:(l + 1) * ns, n_new:], new_win], axis=1)
        for i, a in enumerate((us3[:, :, _US_CMP:_US_CMP + 128].reshape(ns, n_new, 1, 128),
                               us3[:, :, _US_SLC:_US_SLC + 128].reshape(ns, n_new, 1, 128),
                               s_win.reshape(ns, wbuf, 1, 128),
                               lat.reshape(ns, r, MLA_LAT)[:, :n_new],
                               us3[:, :, _US_DIFF_KV:_US_DIFF_KV + 4 * DIFF_DH].reshape(ns, n_new, 1, 4 * DIFF_DH))):
            s_st[i].append(a)
    p_nsa_cmp, p_nsa_slc, p_nsa_win, p_mla_latent, p_diff_kv, p_mem_kv = [jnp.stack(a, axis=0) for a in p_st]
    s_nsa_cmp, s_nsa_slc, s_nsa_win, s_mla_latent, s_diff_kv = [jnp.stack(a, axis=0) for a in s_st]
    y_prompt = xp.reshape(1, t, D_MODEL)
    y_sample = xs.reshape(ns, r, D_MODEL)[:, :n_new]
    return (y_prompt, y_sample, p_nsa_cmp, s_nsa_cmp, p_nsa_slc, s_nsa_slc, p_nsa_win, s_nsa_win,
            p_mla_latent, s_mla_latent, p_diff_kv, s_diff_kv, p_mem_kv)
```

```python
import functools
import math

import numpy as np
import jax
import jax.numpy as jnp
from jax import lax
from jax.experimental import pallas as pl
from jax.experimental.pallas import tpu as pltpu

F32 = jnp.float32
BF16 = jnp.bfloat16

D_MODEL = 1024
PAGE = 128
BRANCH_W = 512
N_BRANCH = 4
NSA_H = 8
NSA_DH = 64
NSA_BLOCK = 64
NSA_TOPN = 16
NSA_WINDOW = 512
MLA_H = 8
MLA_Q_RANK = 256
MLA_KV_RANK = 128
MLA_NOPE = 64
MLA_ROPE = 32
MLA_DV = 64
MLA_LAT = MLA_KV_RANK + MLA_ROPE
ROPE_THETA = 10000.0
DIFF_H = 4
DIFF_DH = 64
N_MEM = 256
MEM_H = 4
MEM_DH = 128
N_BUCKETS = 32
MAX_DISTANCE = 128
BIAS_H = NSA_H + DIFF_H
EPS = 1e-6
NEG = -1e30
TINY = float(np.finfo(np.float32).tiny)

VMEM_LIMIT = 56 * 1024 * 1024

SAMPLE_ROWS = 8
PAGES_PER_STEP = 64
TS = 256
QT = 2 * TS
MLA_QW = 256

_US_NSA_Q, _US_DIFF_Q, _US_MEM_Q, _US_CQ, _US_DIFF_KV = 0, 512, 1024, 1536, 1792
_US_CMP, _US_SLC, _US_WIN, _US_CKV, _US_MISC, _US_W = 2048, 2176, 2304, 2432, 2560, 2688
_MISC_KR, _MISC_KRS = 24, 56


def _bucket_thresholds():
    n = np.arange(0, 4 * MAX_DISTANCE, dtype=np.int64)
    exact = N_BUCKETS // 2
    nf = np.maximum(n, exact).astype(np.float32)
    large = exact + (np.log(nf / np.float32(exact)) / np.float32(math.log(MAX_DISTANCE / exact))
                     * np.float32(N_BUCKETS - exact)).astype(np.int32)
    bucket = np.where(n < exact, n, np.minimum(large, N_BUCKETS - 1))
    return [int(np.argmax(bucket >= b)) for b in range(N_BUCKETS)]


_THR = _bucket_thresholds()


def _cparams(sem):
    return pltpu.CompilerParams(dimension_semantics=sem, vmem_limit_bytes=VMEM_LIMIT)


def _resident(shape):
    nd = len(shape)
    return pl.BlockSpec(shape, lambda *a: (0,) * nd, pipeline_mode=pl.Buffered(1))


def _dot(a, b):
    return jnp.dot(a, b, preferred_element_type=F32)


def _dot_nt(a, b):
    return lax.dot_general(a, b, (((1,), (1,)), ((), ())), preferred_element_type=F32)


def _dot_tn(a, b):
    return lax.dot_general(a, b, (((0,), (0,)), ((), ())), preferred_element_type=F32)


def _rms_rows(x, g):
    return x * lax.rsqrt(jnp.mean(x * x, axis=-1, keepdims=True) + EPS) * g


def _rms_matmul_kernel(x_ref, g_ref, w_ref, o_ref):
    y = _rms_rows(x_ref[...], g_ref[...])
    o_ref[...] = _dot(y.astype(BF16), w_ref[...]).astype(o_ref.dtype)


def _rms_matmul(x, g, w, out_dtype, tm, tn, name):
    m, k = x.shape
    n = w.shape[1]
    return pl.pallas_call(
        _rms_matmul_kernel,
        grid=(n // tn, m // tm),
        in_specs=[pl.BlockSpec((tm, k), lambda j, i: (i, 0)),
                  pl.BlockSpec((1, k), lambda j, i: (0, 0)),
                  pl.BlockSpec((k, tn), lambda j, i: (0, j))],
        out_specs=pl.BlockSpec((tm, tn), lambda j, i: (i, j)),
        out_shape=jax.ShapeDtypeStruct((m, n), out_dtype),
        compiler_params=_cparams(("parallel", "parallel")),
        name=name,
    )(x, g.reshape(1, k), w)


def _bias_expand_kernel(tbl_ref, d_ref, o_ref, *, h0):
    h = pl.program_id(0) + h0
    d = d_ref[...]
    out = jnp.full(d.shape, tbl_ref[0, h], F32)
    for b in range(1, N_BUCKETS):
        out = jnp.where(d >= _THR[b], tbl_ref[b, h], out)
    o_ref[0] = out - tbl_ref[N_BUCKETS - 1, h]


def _bias_expand(tbl, dist, h0, nh, name):
    r, c = dist.shape
    tr = min(r, 512)
    return pl.pallas_call(
        functools.partial(_bias_expand_kernel, h0=h0),
        grid=(nh, r // tr),
        in_specs=[pl.BlockSpec(memory_space=pltpu.SMEM),
                  pl.BlockSpec((tr, c), lambda h, i: (i, 0))],
        out_specs=pl.BlockSpec((1, tr, c), lambda h, i: (h, i, 0)),
        out_shape=jax.ShapeDtypeStruct((nh, r, c), F32),
        compiler_params=_cparams(("parallel", "parallel")),
        name=name,
    )(tbl, dist)


def _mla_prep_kernel(cq_ref, ckv_ref, misc_ref, cos_ref, sin_ref, gq_ref, gkv_ref,
                     wn_ref, wr_ref, wrs_ref, wuk_ref, q_ref, lat_ref, *, rope_first):
    scale = (MLA_NOPE + MLA_ROPE) ** -0.5
    cqn = _rms_rows(cq_ref[...], gq_ref[...]).astype(BF16)
    qn = _dot(cqn, wn_ref[...])
    qr = _dot(cqn, wr_ref[...])
    qrs = _dot(cqn, wrs_ref[...])
    cos = cos_ref[...]
    sin = sin_ref[...]
    qw = q_ref.shape[-1]
    lat0, rope0 = (qw - MLA_KV_RANK, 0) if rope_first else (0, MLA_KV_RANK)
    if rope_first:
        q_ref[...] = jnp.zeros(q_ref.shape, BF16)
    for h in range(MLA_H):
        ql = _dot(qn[:, h * MLA_NOPE:(h + 1) * MLA_NOPE].astype(BF16), wuk_ref[h])
        q_ref[h, :, lat0:lat0 + MLA_KV_RANK] = (ql * scale).astype(BF16)
        sl = slice(h * MLA_ROPE, (h + 1) * MLA_ROPE)
        rope = qr[:, sl] * cos + qrs[:, sl] * sin
        q_ref[h, :, rope0:rope0 + MLA_ROPE] = (rope * scale).astype(BF16)
    lat_ref[:, 0:MLA_KV_RANK] = _rms_rows(ckv_ref[...], gkv_ref[...])
    misc = misc_ref[...]
    kr = misc[:, _MISC_KR:_MISC_KR + MLA_ROPE]
    krs = misc[:, _MISC_KRS:_MISC_KRS + MLA_ROPE]
    lat_ref[:, MLA_KV_RANK:MLA_LAT] = kr * cos + krs * sin


def _mla_prep(us, cos, sin, lw, tm, rope_first, name):
    m = us.shape[0]
    qw = MLA_QW if rope_first else MLA_LAT
    col = lambda off, w: pl.BlockSpec((tm, w), lambda i: (i, off // w))
    return pl.pallas_call(
        functools.partial(_mla_prep_kernel, rope_first=rope_first),
        grid=(m // tm,),
        in_specs=[col(_US_CQ, MLA_Q_RANK), col(_US_CKV, MLA_KV_RANK), col(_US_MISC, 128),
                  pl.BlockSpec((tm, MLA_ROPE), lambda i: (i, 0)),
                  pl.BlockSpec((tm, MLA_ROPE), lambda i: (i, 0)),
                  _resident((1, MLA_Q_RANK)), _resident((1, MLA_KV_RANK)),
                  _resident((MLA_Q_RANK, MLA_H * MLA_NOPE)),
                  _resident((MLA_Q_RANK, MLA_H * MLA_ROPE)),
                  _resident((MLA_Q_RANK, MLA_H * MLA_ROPE)),
                  _resident((MLA_H, MLA_NOPE, MLA_KV_RANK))],
        out_specs=[pl.BlockSpec((MLA_H, tm, qw), lambda i: (0, i, 0)),
                   pl.BlockSpec((tm, MLA_LAT), lambda i: (i, 0))],
        out_shape=[jax.ShapeDtypeStruct((MLA_H, m, qw), BF16),
                   jax.ShapeDtypeStruct((m, MLA_LAT), F32)],
        compiler_params=_cparams(("parallel",)),
        name=name,
    )(us, us, us, cos, sin, lw['gq'], lw['gkv'], lw['w_uq_n'], lw['w_uq_r'], lw['w_uq_rs'], lw['w_ukT'])


def _pad_heads_128(q, n_heads, scale):
    r = q.shape[0]
    lane = lax.broadcasted_iota(jnp.int32, (r, 128), 1)
    parts = []
    for h in range(n_heads):
        slab = q[:, (h // 2) * 128:(h // 2 + 1) * 128]
        if h % 2 == 1:
            slab = pltpu.roll(slab, 64, 1)
        parts.append(jnp.where(lane < 64, slab * scale, 0.0).astype(BF16))
    return jnp.concatenate(parts, axis=0)


def _flash_init(s, v, m_ref, l_ref, acc_ref, w=None):
    m = jnp.max(s, axis=-1, keepdims=True)
    p = jnp.exp(s - m)
    if w is not None:
        p = p * w
    m_ref[...] = m
    l_ref[...] = jnp.sum(p, axis=-1, keepdims=True)
    acc_ref[...] = _dot(p.astype(BF16), v)


def _flash_step(s, v, m_ref, l_ref, acc_ref, w=None, v_feature_major=False):
    m_prev = m_ref[...]
    m_new = jnp.maximum(m_prev, jnp.max(s, axis=-1, keepdims=True))
    a = jnp.exp(m_prev - m_new)
    p = jnp.exp(s - m_new)
    if w is not None:
        p = p * w
    l_ref[...] = a * l_ref[...] + jnp.sum(p, axis=-1, keepdims=True)
    pv = _dot_nt(p.astype(BF16), v) if v_feature_major else _dot(p.astype(BF16), v)
    acc_ref[...] = a * acc_ref[...] + pv
    m_ref[...] = m_new


def _heads3(x, h):
    return x.reshape(h, x.shape[0] // h, x.shape[1])


def _sigmoid(x):
    return 1.0 / (1.0 + jnp.exp(-x))


def _nsa_cmp_kernel(q_ref, misc_ref, kvc_ref, bias_ref, qpos_ref, oc_ref, cnt_ref, *, nb_true):
    r = q_ref.shape[0]
    nbk = kvc_ref.shape[-2]
    kv = kvc_ref[...].reshape(nbk, 128).astype(BF16)
    qp = _pad_heads_128(q_ref[...], NSA_H, NSA_DH ** -0.5)
    qpos = qpos_ref[...]
    blk = lax.broadcasted_iota(jnp.int32, (r, nbk), 1)
    valid = jnp.logical_and(blk * NSA_BLOCK + (NSA_BLOCK - 1) <= qpos, blk < nb_true)
    s = _heads3(_dot_nt(qp, kv), NSA_H) + bias_ref[...]
    s = jnp.where(valid[None], s, NEG)
    m = jnp.max(s, axis=-1, keepdims=True)
    e = jnp.where(valid[None], jnp.exp(s - m), 0.0)
    p = e / jnp.maximum(jnp.sum(e, axis=-1, keepdims=True), TINY)
    o = _dot(p.reshape(NSA_H * r, nbk).astype(BF16), kv)
    g0 = _sigmoid(misc_ref[:, 0:NSA_H])
    pieces = [o[h * r:(h + 1) * r, NSA_DH:] * g0[:, h:h + 1] for h in range(NSA_H)]
    oc_ref[...] = jnp.concatenate(pieces, axis=1)
    work = jnp.where(valid, jnp.sum(p, axis=0), -1.0)
    cur = lax.shift_right_arithmetic(qpos, int(math.log2(NSA_BLOCK)))
    cnt = jnp.where(blk == cur, 1.0, 0.0)
    blkf = blk.astype(F32)
    for _ in range(min(NSA_TOPN, nb_true)):
        mx = jnp.max(work, axis=-1, keepdims=True)
        idx = jnp.min(jnp.where(work == mx, blkf, float(nbk)), axis=-1, keepdims=True)
        sel = blkf == idx
        cnt = cnt + jnp.where(jnp.logical_and(sel, mx >= 0.0), 1.0, 0.0)
        work = jnp.where(sel, -2.0, work)
    cnt_ref[...] = cnt.astype(BF16)


def _nsa_cmp(us, kvc, bias_c, qpos, nb_true, rows, per_seq, name):
    m = us.shape[0]
    nbk = kvc.shape[-2]
    col = lambda off, w: pl.BlockSpec((rows, w), lambda i: (i, off // w))
    if per_seq:
        kvc_spec = pl.BlockSpec((1, nbk, 128), lambda i: (i, 0, 0))
        bias_spec = _resident((NSA_H, rows, nbk))
    else:
        kvc_spec = _resident((1, nbk, 128))
        bias_spec = pl.BlockSpec((NSA_H, rows, nbk), lambda i: (0, i, 0))
    return pl.pallas_call(
        functools.partial(_nsa_cmp_kernel, nb_true=nb_true),
        grid=(m // rows,),
        in_specs=[col(_US_NSA_Q, 512), col(_US_MISC, 128), kvc_spec, bias_spec,
                  pl.BlockSpec((rows, 1), lambda i: (i, 0))],
        out_specs=[pl.BlockSpec((rows, 512), lambda i: (i, 0)),
                   pl.BlockSpec((rows, nbk), lambda i: (i, 0))],
        out_shape=[jax.ShapeDtypeStruct((m, 512), F32), jax.ShapeDtypeStruct((m, nbk), BF16)],
        compiler_params=_cparams(("parallel",)),
        name=name,
    )(us, us, kvc, bias_c, qpos)


def _pool_kernel(x_ref, o_ref):
    x = x_ref[...]
    nb = x.shape[0] // NSA_BLOCK
    o_ref[0] = jnp.sum(x.reshape(nb, NSA_BLOCK, 128), axis=1) * (1.0 / NSA_BLOCK)


def _pool_prompt(us, name):
    t = us.shape[0]
    rows = min(t, 2048)
    nb = t // NSA_BLOCK
    return pl.pallas_call(
        _pool_kernel,
        grid=(t // rows,),
        in_specs=[pl.BlockSpec((rows, 128), lambda i: (i, _US_CMP // 128))],
        out_specs=pl.BlockSpec((1, rows // NSA_BLOCK, 128), lambda i: (0, i, 0)),
        out_shape=jax.ShapeDtypeStruct((1, nb, 128), F32),
        compiler_params=_cparams(("parallel",)),
        name=name,
    )(us)


def _rowmax(tiles):
    m = None
    for s in tiles:
        for i in range(s.shape[1] // 128):
            c = s[:, i * 128:(i + 1) * 128]
            m = c if m is None else jnp.maximum(m, c)
    return jnp.max(m, axis=-1, keepdims=True)


def _lane_tile(x, width):
    n = width // x.shape[1]
    return x if n == 1 else jnp.concatenate([x] * n, axis=1)


def _flash_rows(s_tiles, v, m_ref, acc_ref, idx, w_tiles=None):
    m_prev = m_ref[idx]
    m_new = jnp.maximum(m_prev, _rowmax(s_tiles))
    a = jnp.exp(m_prev - m_new)
    ps = []
    for i, s in enumerate(s_tiles):
        p = jnp.exp(s - _lane_tile(m_new, s.shape[1]))
        if w_tiles is not None:
            p = p * w_tiles[i]
        ps.append(p.astype(BF16))
    p = ps[0] if len(ps) == 1 else jnp.concatenate(ps, axis=1)
    acc = acc_ref[idx]
    acc_ref[idx] = _lane_tile(a, acc.shape[1]) * acc + _dot(p, v)
    m_ref[idx] = m_new


def _causal_tile():
    row = lax.broadcasted_iota(jnp.int32, (TS, TS), 0)
    colk = lax.broadcasted_iota(jnp.int32, (TS, TS), 1)
    return row, colk


def _rows(ref, start, size):
    return ref[pl.ds(pl.multiple_of(start, TS), size), :]


def _mla_prompt_kernel(q_ref, kt_ref, v_ref, wuv_ref, o_ref, m_ref, acc_ref):
    qi = pl.program_id(0)
    m_ref[...] = jnp.full(m_ref.shape, NEG, F32)
    acc_ref[...] = jnp.zeros(acc_ref.shape, F32)

    def qblk(u, h):
        return q_ref[h, u * TS:(u + 1) * TS, :]

    def far(c, carry):
        kts = [kt_ref[2 * c], kt_ref[2 * c + 1]]
        v = _rows(v_ref, c * QT, QT)
        for u in range(2):
            for h in range(MLA_H):
                q = qblk(u, h)
                _flash_rows([_dot(q, kt) for kt in kts], v, m_ref, acc_ref, u * MLA_H + h)
        return carry

    lax.fori_loop(0, qi, far, 0)

    row, colk = _causal_tile()
    causal = colk <= row
    kt_a, kt_b = kt_ref[2 * qi], kt_ref[2 * qi + 1]
    v_ab = _rows(v_ref, qi * QT, QT)
    for h in range(MLA_H):
        s = jnp.where(causal, _dot(qblk(0, h), kt_a), NEG)
        _flash_rows([s], v_ab[:TS], m_ref, acc_ref, h)
    for h in range(MLA_H):
        q = qblk(1, h)
        s1 = jnp.where(causal, _dot(q, kt_b), NEG)
        _flash_rows([_dot(q, kt_a), s1], v_ab, m_ref, acc_ref, MLA_H + h)

    for u in range(2):
        pieces = []
        for h in range(MLA_H):
            acc = acc_ref[u * MLA_H + h]
            o_lat = acc[:, :MLA_KV_RANK] / acc[:, MLA_KV_RANK:]
            pieces.append(_dot(o_lat.astype(BF16), wuv_ref[h]))
        o_ref[u * TS:(u + 1) * TS, :] = jnp.concatenate(pieces, axis=1)


def _mla_prompt(q_mla, kt, v_ext, wuv, name):
    t = v_ext.shape[0]
    nblk = 2 * MLA_H
    return pl.pallas_call(
        _mla_prompt_kernel,
        grid=(t // QT,),
        in_specs=[pl.BlockSpec((MLA_H, QT, MLA_QW), lambda i: (0, i, 0)),
                  _resident(kt.shape), _resident(v_ext.shape), _resident((MLA_H, MLA_KV_RANK, MLA_DV))],
        out_specs=pl.BlockSpec((QT, 512), lambda i: (i, 0)),
        out_shape=jax.ShapeDtypeStruct((t, 512), F32),
        scratch_shapes=[pltpu.VMEM((nblk, TS, 128), F32), pltpu.VMEM((nblk, TS, 2 * MLA_KV_RANK), F32)],
        compiler_params=_cparams(("parallel",)),
        name=name,
    )(q_mla, kt, v_ext, wuv)


def _diff_queries(q):
    r = q.shape[0]
    lane = lax.broadcasted_iota(jnp.int32, (r, 128), 1)
    qs = q * (DIFF_DH ** -0.5)
    m1 = [jnp.where(lane < DIFF_DH, qs[:, h * 128:(h + 1) * 128], 0.0).astype(BF16) for h in range(DIFF_H)]
    m2 = [jnp.where(lane >= DIFF_DH, qs[:, h * 128:(h + 1) * 128], 0.0).astype(BF16) for h in range(DIFF_H)]
    return jnp.concatenate(m1 + m2, axis=0)


def _diff_lambda(lp_ref, lam_init):
    lp = lp_ref[...]
    return (jnp.exp(jnp.sum(lp[0:1] * lp[1:2], axis=-1, keepdims=True))
            - jnp.exp(jnp.sum(lp[2:3] * lp[3:4], axis=-1, keepdims=True)) + lam_init)


def _diff_combine(o1, o2, lam, g, lam_init):
    return _rms_rows(o1 - lam * o2, g) * (1.0 - lam_init)


def _diff_finish(acc, l, lp_ref, g_ref, o_ref, rows, lam_init):
    lam = _diff_lambda(lp_ref, lam_init)
    o = acc / l
    half = DIFF_H * rows
    y = _diff_combine(o[:half], o[half:], lam, g_ref[...], lam_init)
    o_ref[...] = jnp.concatenate([y[h * rows:(h + 1) * rows] for h in range(DIFF_H)], axis=1)


def _stage_queries(qs_ref, make, q_ref, nblk):
    for u in range(2):
        q = make(q_ref[u * TS:(u + 1) * TS, :])
        for r in range(nblk):
            qs_ref[u * nblk + r] = q[r * TS:(r + 1) * TS]


def _diff_prompt_kernel(q_ref, kt_ref, v_ref, bt_ref, lp_ref, g_ref, o_ref, qs_ref, m_ref, acc_ref, *, lam_init):
    qi = pl.program_id(0)
    nblk = 2 * DIFF_H
    m_ref[...] = jnp.full(m_ref.shape, NEG, F32)
    acc_ref[...] = jnp.zeros(acc_ref.shape, F32)
    _stage_queries(qs_ref, _diff_queries, q_ref, nblk)

    def step(c, near_for_a):
        kts = [kt_ref[2 * c], kt_ref[2 * c + 1]]
        v = _rows(v_ref, c * QT, QT)
        for u in range(2):
            for r in range(nblk):
                q = qs_ref[u * nblk + r]
                s0, s1 = _dot(q, kts[0]), _dot(q, kts[1])
                if near_for_a and u == 0:
                    s1 = s1 + bt_ref[r % DIFF_H, 1]
                _flash_rows([s0, s1], v, m_ref, acc_ref, u * nblk + r)

    def far(c, carry):
        step(c, False)
        return carry

    lax.fori_loop(0, jnp.maximum(qi - 1, 0), far, 0)

    @pl.when(qi >= 1)
    def _():
        step(qi - 1, True)

    row, colk = _causal_tile()
    causal = colk <= row
    kt_a, kt_b = kt_ref[2 * qi], kt_ref[2 * qi + 1]
    v_ab = _rows(v_ref, qi * QT, QT)
    for r in range(nblk):
        s = jnp.where(causal, _dot(qs_ref[r], kt_a) + bt_ref[r % DIFF_H, 0], NEG)
        _flash_rows([s], v_ab[:TS], m_ref, acc_ref, r)
    for r in range(nblk):
        q = qs_ref[nblk + r]
        s0 = _dot(q, kt_a) + bt_ref[r % DIFF_H, 1]
        s1 = jnp.where(causal, _dot(q, kt_b) + bt_ref[r % DIFF_H, 0], NEG)
        _flash_rows([s0, s1], v_ab, m_ref, acc_ref, nblk + r)

    lam = _diff_lambda(lp_ref, lam_init)
    g = g_ref[...]
    for u in range(2):
        pieces = []
        for h in range(DIFF_H):
            a1 = acc_ref[u * nblk + h]
            a2 = acc_ref[u * nblk + DIFF_H + h]
            o1 = a1[:, :2 * DIFF_DH] / a1[:, 2 * DIFF_DH:]
            o2 = a2[:, :2 * DIFF_DH] / a2[:, 2 * DIFF_DH:]
            pieces.append(_diff_combine(o1, o2, lam, g, lam_init))
        o_ref[u * TS:(u + 1) * TS, :] = jnp.concatenate(pieces, axis=1)


def _diff_prompt(us, kt, v_ext, bt, lp, g, lam_init, name):
    t = us.shape[0]
    nblk = 2 * 2 * DIFF_H
    return pl.pallas_call(
        functools.partial(_diff_prompt_kernel, lam_init=lam_init),
        grid=(t // QT,),
        in_specs=[pl.BlockSpec((QT, 512), lambda i: (i, _US_DIFF_Q // 512)),
                  _resident(kt.shape), _resident(v_ext.shape), _resident((DIFF_H, 2, TS, TS)),
                  _resident((4, DIFF_DH)), _resident((1, 2 * DIFF_DH))],
        out_specs=pl.BlockSpec((QT, 512), lambda i: (i, 0)),
        out_shape=jax.ShapeDtypeStruct((t, 512), F32),
        scratch_shapes=[pltpu.VMEM((nblk, TS, 128), BF16), pltpu.VMEM((nblk, TS, 128), F32),
                        pltpu.VMEM((nblk, TS, 4 * DIFF_DH), F32)],
        compiler_params=_cparams(("parallel",)),
        name=name,
    )(us, kt, v_ext, bt, lp, g)


def _nsa_prompt_kernel(q_ref, misc_ref, oc_ref, cnt_ref, kt_ref, v_ref, wkt_ref, wv_ref, bt_ref, ex_ref, o_ref,
                       qs_ref, m_ref, acc_ref):
    qi = pl.program_id(0)
    nblk = NSA_H
    m_ref[...] = jnp.full(m_ref.shape, NEG, F32)
    acc_ref[...] = jnp.zeros(acc_ref.shape, F32)
    _stage_queries(qs_ref, lambda q: _pad_heads_128(q, NSA_H, NSA_DH ** -0.5), q_ref, nblk)
    ex = ex_ref[...]

    def mult(u, j):
        return _dot_tn(cnt_ref[j][:, u * TS:(u + 1) * TS], ex)

    def step(c, near_for_a):
        kts = [kt_ref[2 * c], kt_ref[2 * c + 1]]
        v = _rows(v_ref, c * QT, QT)
        for u in range(2):
            w0, w1 = mult(u, 2 * c), mult(u, 2 * c + 1)
            ok0, ok1 = w0 > 0.0, w1 > 0.0
            for h in range(nblk):
                q = qs_ref[u * nblk + h]
                s0, s1 = _dot(q, kts[0]), _dot(q, kts[1])
                if near_for_a and u == 0:
                    s1 = s1 + bt_ref[h, 1]
                _flash_rows([jnp.where(ok0, s0, NEG), jnp.where(ok1, s1, NEG)], v, m_ref, acc_ref,
                            u * nblk + h, [w0, w1])

    def far(c, carry):
        step(c, False)
        return carry

    lax.fori_loop(0, jnp.maximum(qi - 1, 0), far, 0)

    @pl.when(qi >= 1)
    def _():
        step(qi - 1, True)

    row, colk = _causal_tile()
    causal = colk <= row
    kt_a, kt_b = kt_ref[2 * qi], kt_ref[2 * qi + 1]
    v_ab = _rows(v_ref, qi * QT, QT)
    w = mult(0, 2 * qi)
    ok = jnp.logical_and(causal, w > 0.0)
    for h in range(nblk):
        s = jnp.where(ok, _dot(qs_ref[h], kt_a) + bt_ref[h, 0], NEG)
        _flash_rows([s], v_ab[:TS], m_ref, acc_ref, h, [w])
    w0, w1 = mult(1, 2 * qi), mult(1, 2 * qi + 1)
    ok0, ok1 = w0 > 0.0, jnp.logical_and(causal, w1 > 0.0)
    for h in range(nblk):
        q = qs_ref[nblk + h]
        s0 = jnp.where(ok0, _dot(q, kt_a) + bt_ref[h, 1], NEG)
        s1 = jnp.where(ok1, _dot(q, kt_b) + bt_ref[h, 0], NEG)
        _flash_rows([s0, s1], v_ab, m_ref, acc_ref, nblk + h, [w0, w1])

    g = _sigmoid(misc_ref[:, 0:3 * NSA_H])
    oc = oc_ref[...]
    for u in range(2):
        t_abs = 2 * qi + u
        j2, j1 = jnp.maximum(t_abs - 2, 0), jnp.maximum(t_abs - 1, 0)
        mask2 = jnp.logical_and(colk >= row, t_abs >= 2)
        mask1 = jnp.logical_and(colk >= 0, t_abs >= 1)
        kt2, kt1, kt0 = wkt_ref[j2], wkt_ref[j1], wkt_ref[t_abs]
        v2, v1, v0 = _rows(wv_ref, j2 * TS, TS), _rows(wv_ref, j1 * TS, TS), _rows(wv_ref, t_abs * TS, TS)
        rs = slice(u * TS, (u + 1) * TS)
        pieces = []
        for h in range(nblk):
            q = qs_ref[u * nblk + h]
            s2 = jnp.where(mask2, _dot(q, kt2), NEG)
            s1 = jnp.where(mask1, _dot(q, kt1) + bt_ref[h, 1], NEG)
            s0 = jnp.where(causal, _dot(q, kt0) + bt_ref[h, 0], NEG)
            mw = _rowmax([s2, s1, s0])
            p2, p1, p0 = jnp.exp(s2 - mw), jnp.exp(s1 - mw), jnp.exp(s0 - mw)
            lw = jnp.sum(p2 + p1 + p0, axis=-1, keepdims=True)
            ow = (_dot(p2.astype(BF16), v2) + _dot(p1.astype(BF16), v1) + _dot(p0.astype(BF16), v0)) / lw
            acc = acc_ref[u * nblk + h]
            o_s = acc[:, NSA_DH:2 * NSA_DH] / acc[:, 2 * NSA_DH:3 * NSA_DH]
            pieces.append(oc[rs, h * NSA_DH:(h + 1) * NSA_DH]
                          + g[rs, NSA_H + h:NSA_H + h + 1] * o_s
                          + g[rs, 2 * NSA_H + h:2 * NSA_H + h + 1] * ow[:, NSA_DH:])
        o_ref[rs, :] = jnp.concatenate(pieces, axis=1)


def _nsa_prompt(us, oc, cnt8, kt, v_ext, wkt, wv, bt, ex8, name):
    t = us.shape[0]
    nblk = 2 * NSA_H
    col = lambda off, w: pl.BlockSpec((QT, w), lambda i: (i, off // w))
    return pl.pallas_call(
        _nsa_prompt_kernel,
        grid=(t // QT,),
        in_specs=[col(_US_NSA_Q, 512), col(_US_MISC, 128),
                  pl.BlockSpec((QT, 512), lambda i: (i, 0)),
                  pl.BlockSpec((cnt8.shape[0], 8, QT), lambda i: (0, 0, i)),
                  _resident(kt.shape), _resident(v_ext.shape), _resident(wkt.shape), _resident(wv.shape),
                  _resident((NSA_H, 2, TS, TS)), _resident(ex8.shape)],
        out_specs=pl.BlockSpec((QT, 512), lambda i: (i, 0)),
        out_shape=jax.ShapeDtypeStruct((t, 512), F32),
        scratch_shapes=[pltpu.VMEM((nblk, TS, 128), BF16), pltpu.VMEM((nblk, TS, 128), F32),
                        pltpu.VMEM((nblk, TS, 256), F32)],
        compiler_params=_cparams(("parallel",)),
        name=name,
    )(us, us, oc, cnt8, kt, v_ext, wkt, wv, bt, ex8)


def _key_tiles(x):
    t, d = x.shape
    return x.reshape(t // TS, TS, d).transpose(0, 2, 1)


def _with_ones(x, n):
    return jnp.concatenate([x, jnp.ones((x.shape[0], n), x.dtype)], axis=1)


def _mem_kernel(q_ref, mkv_ref, o_ref):
    q = q_ref[...]
    outs = []
    for h in range(MEM_H):
        qh = q[:, h * MEM_DH:(h + 1) * MEM_DH].astype(BF16)
        kvh = mkv_ref[0, :, h * 2 * MEM_DH:(h + 1) * 2 * MEM_DH].astype(BF16)
        k, v = kvh[:, :MEM_DH], kvh[:, MEM_DH:]
        s = _dot_nt(qh, k) * (MEM_DH ** -0.5)
        m = jnp.max(s, axis=-1, keepdims=True)
        e = jnp.exp(s - m)
        p = e / jnp.sum(e, axis=-1, keepdims=True)
        outs.append(_dot(p.astype(BF16), v))
    o_ref[...] = jnp.concatenate(outs, axis=1)


def _mem_attend(us, mkv, rows, per_seq, name):
    m = us.shape[0]
    if per_seq:
        mkv_spec = pl.BlockSpec((1,) + mkv.shape[1:], lambda i: (i, 0, 0))
    else:
        mkv_spec = _resident(mkv.shape)
    return pl.pallas_call(
        _mem_kernel,
        grid=(m // rows,),
        in_specs=[pl.BlockSpec((rows, 512), lambda i: (i, _US_MEM_Q // 512)), mkv_spec],
        out_specs=pl.BlockSpec((rows, 512), lambda i: (i, 0)),
        out_shape=jax.ShapeDtypeStruct((m, 512), F32),
        compiler_params=_cparams(("parallel",)),
        name=name,
    )(us, mkv)


def _merge_kernel(x_ref, o0_ref, o1_ref, o2_ref, o3_ref, z_ref, mg_ref, wb_ref, wo_ref, g_ref, y_ref):
    h = None
    for n, o_ref in enumerate((o0_ref, o1_ref, o2_ref, o3_ref)):
        z = z_ref[:, n * BRANCH_W:(n + 1) * BRANCH_W].astype(F32)
        gated = o_ref[...] * (z * _sigmoid(z))
        br = _dot(gated.astype(BF16), wb_ref[n])
        term = _sigmoid(mg_ref[:, n * D_MODEL:(n + 1) * D_MODEL].astype(F32)) * br
        h = term if h is None else h + term
    y = _dot(h.astype(BF16), wo_ref[...])
    y_ref[...] = x_ref[...] + _rms_rows(y, g_ref[...])


def _merge(x, outs, ub, lw, tm, name):
    m = x.shape[0]
    row = lambda w: pl.BlockSpec((tm, w), lambda i: (i, 0))
    return pl.pallas_call(
        _merge_kernel,
        grid=(m // tm,),
        in_specs=[row(D_MODEL), row(512), row(512), row(512), row(512),
                  pl.BlockSpec((tm, N_BRANCH * BRANCH_W), lambda i: (i, N_BRANCH * D_MODEL // (N_BRANCH * BRANCH_W))),
                  pl.BlockSpec((tm, N_BRANCH * D_MODEL), lambda i: (i, 0)),
                  _resident((N_BRANCH, BRANCH_W, D_MODEL)), _resident((D_MODEL, D_MODEL)),
                  _resident((1, D_MODEL))],
        out_specs=row(D_MODEL),
        out_shape=jax.ShapeDtypeStruct((m, D_MODEL), F32),
        compiler_params=_cparams(("parallel",)),
        name=name,
    )(x, *outs, ub, ub, lw['w_branch'], lw['w_out'], lw['post_g'])


def _page_specs(rows, width, n_pages, pps):
    def spec(j):
        return pl.BlockSpec((1, rows, width), lambda b, c, pt: (pt[b * n_pages + c * pps + j], 0, 0))
    return [spec(j) for j in range(pps)]


def _gather_pages(page_refs):
    return jnp.concatenate([r[0] for r in page_refs], axis=0)


def _pool_paged_kernel(pt_ref, *refs):
    pps = len(refs) - 3
    new_ref, o_ref, x_ref = refs[pps], refs[pps + 1], refs[pps + 2]
    means = []
    for r in refs[:pps]:
        means.append(jnp.sum(r[0].reshape(PAGE // NSA_BLOCK, NSA_BLOCK, 128), axis=1) * (1.0 / NSA_BLOCK))
    o_ref[0] = jnp.concatenate(means, axis=0)
    part = jnp.sum(new_ref[...], axis=0, keepdims=True) * (1.0 / NSA_BLOCK)
    rowi = lax.broadcasted_iota(jnp.int32, (128, 128), 0)
    x_ref[0] = jnp.where(rowi == 0, jnp.broadcast_to(part, (128, 128)), 0.0)


def _pool_paged(pt, cache, us, n_seq, n_pages, pps, name):
    nbp = n_pages * (PAGE // NSA_BLOCK)
    per = pps * (PAGE // NSA_BLOCK)
    grid_spec = pltpu.PrefetchScalarGridSpec(
        num_scalar_prefetch=1,
        grid=(n_seq, n_pages // pps),
        in_specs=_page_specs(PAGE, 128, n_pages, pps)
        + [pl.BlockSpec((SAMPLE_ROWS, 128), lambda b, c, pt: (b, _US_CMP // 128))],
        out_specs=[pl.BlockSpec((1, per, 128), lambda b, c, pt: (b, c, 0)),
                   pl.BlockSpec((1, 128, 128), lambda b, c, pt: (b, 0, 0))],
    )
    return pl.pallas_call(
        _pool_paged_kernel,
        grid_spec=grid_spec,
        out_shape=[jax.ShapeDtypeStruct((n_seq, nbp, 128), F32),
                   jax.ShapeDtypeStruct((n_seq, 128, 128), F32)],
        compiler_params=_cparams(("parallel", "arbitrary")),
        name=name,
    )(pt, *([cache] * pps), us)


def _new_row_mask(n_new):
    row = lax.broadcasted_iota(jnp.int32, (SAMPLE_ROWS, SAMPLE_ROWS), 0)
    colk = lax.broadcasted_iota(jnp.int32, (SAMPLE_ROWS, SAMPLE_ROWS), 1)
    return jnp.logical_and(colk <= row, colk < n_new)


def _nsa_paged_kernel(pt_ref, *refs, n_new, cur_blk):
    pps = len(refs) - 16
    pages = refs[:pps]
    (q_ref, misc_ref, oc_ref, cnt_ref, cntx_ref, new_slc_ref, new_win_ref, cwin_ref,
     bl_ref, bn_ref, bw_ref, ex_ref, o_ref, ms, ls, accs) = refs[pps:]
    c = pl.program_id(1)
    nc = pl.num_programs(1)
    r = SAMPLE_ROWS
    hr = NSA_H * r
    qp = _pad_heads_128(q_ref[...], NSA_H, NSA_DH ** -0.5)

    @pl.when(c == 0)
    def _():
        mult = cntx_ref[:, cur_blk:cur_blk + 1].astype(F32)
        kv = new_slc_ref[...].astype(BF16)
        ok = jnp.logical_and(_new_row_mask(n_new), mult > 0.0)
        s = _heads3(_dot_nt(qp, kv), NSA_H) + bn_ref[...]
        s = jnp.where(ok[None], s, NEG).reshape(hr, r)
        w = jnp.broadcast_to(mult[None], (NSA_H, r, r)).reshape(hr, r)
        _flash_init(s, kv, ms, ls, accs, w)

    kv = _gather_pages(pages).astype(BF16)
    w = _dot(cnt_ref[0, 0], ex_ref[...])
    wfull = jnp.broadcast_to(w[None], (NSA_H,) + w.shape).reshape(hr, w.shape[1])
    s3 = _heads3(_dot_nt(qp, kv), NSA_H)

    @pl.when(c < nc - 1)
    def _():
        s = jnp.where((w > 0.0)[None], s3, NEG).reshape(hr, w.shape[1])
        _flash_step(s, kv, ms, ls, accs, wfull)

    @pl.when(c == nc - 1)
    def _():
        s = jnp.where((w > 0.0)[None], s3 + bl_ref[...], NEG).reshape(hr, w.shape[1])
        _flash_step(s, kv, ms, ls, accs, wfull)
        o_s = accs[...] / ls[...]
        kvw = cwin_ref[0].astype(BF16)
        wb = cwin_ref.shape[1]
        row = lax.broadcasted_iota(jnp.int32, (r, wb), 0)
        colk = lax.broadcasted_iota(jnp.int32, (r, wb), 1)
        sw = _heads3(_dot_nt(qp, kvw), NSA_H) + bw_ref[...]
        sw = jnp.where((colk >= row)[None], sw, NEG).reshape(hr, wb)
        kvn = new_win_ref[...].astype(BF16)
        sn = _heads3(_dot_nt(qp, kvn), NSA_H) + bn_ref[...]
        sn = jnp.where(_new_row_mask(n_new)[None], sn, NEG).reshape(hr, r)
        m = jnp.maximum(jnp.max(sw, axis=-1, keepdims=True), jnp.max(sn, axis=-1, keepdims=True))
        pw = jnp.exp(sw - m)
        pn = jnp.exp(sn - m)
        den = jnp.sum(pw, axis=-1, keepdims=True) + jnp.sum(pn, axis=-1, keepdims=True)
        o_w = (_dot(pw.astype(BF16), kvw) + _dot(pn.astype(BF16), kvn)) / den
        g = _sigmoid(misc_ref[:, 0:3 * NSA_H])
        oc = oc_ref[...]
        pieces = []
        for h in range(NSA_H):
            sl = slice(h * r, (h + 1) * r)
            pieces.append(oc[:, h * NSA_DH:(h + 1) * NSA_DH]
                          + g[:, NSA_H + h:NSA_H + h + 1] * o_s[sl, NSA_DH:]
                          + g[:, 2 * NSA_H + h:2 * NSA_H + h + 1] * o_w[sl, NSA_DH:])
        o_ref[...] = jnp.concatenate(pieces, axis=1)


def _nsa_paged(pt, cache, us, oc, cnt4, cntx, cwin, seq0, bl, bn, bw, ex, n_seq, n_pages, pps, n_new, name):
    r = SAMPLE_ROWS
    hr = NSA_H * r
    ck = pps * PAGE
    col = lambda off, w: pl.BlockSpec((r, w), lambda b, c, pt: (b, off // w))
    res = lambda shape: pl.BlockSpec(shape, lambda b, c, pt: (0,) * len(shape))
    nbk = cntx.shape[1]
    grid_spec = pltpu.PrefetchScalarGridSpec(
        num_scalar_prefetch=1,
        grid=(n_seq, n_pages // pps),
        in_specs=_page_specs(PAGE, 128, n_pages, pps) + [
            col(_US_NSA_Q, 512), col(_US_MISC, 128),
            pl.BlockSpec((r, 512), lambda b, c, pt: (b, 0)),
            pl.BlockSpec((1, 1, r, cnt4.shape[3]), lambda b, c, pt: (b, c, 0, 0)),
            pl.BlockSpec((r, nbk), lambda b, c, pt: (b, 0)),
            col(_US_SLC, 128), col(_US_WIN, 128),
            pl.BlockSpec((1,) + cwin.shape[1:], lambda b, c, pt: (b + seq0, 0, 0)),
            res((NSA_H, r, ck)), res((NSA_H, r, r)), res((NSA_H, r, cwin.shape[1])), res(ex.shape)],
        out_specs=pl.BlockSpec((r, 512), lambda b, c, pt: (b, 0)),
        scratch_shapes=[pltpu.VMEM((hr, 1), F32), pltpu.VMEM((hr, 1), F32), pltpu.VMEM((hr, 128), F32)],
    )
    cur_blk = n_pages * (PAGE // NSA_BLOCK)
    return pl.pallas_call(
        functools.partial(_nsa_paged_kernel, n_new=n_new, cur_blk=cur_blk),
        grid_spec=grid_spec,
        out_shape=jax.ShapeDtypeStruct((n_seq * r, 512), F32),
        compiler_params=_cparams(("parallel", "arbitrary")),
        name=name,
    )(pt, *([cache] * pps), us, us, oc, cnt4, cntx, us, us, cwin, bl, bn, bw, ex)


def _mla_finish(acc, l, wuv_ref, o_ref, rows):
    o_lat = acc / l
    pieces = [_dot(o_lat[h * rows:(h + 1) * rows].astype(BF16), wuv_ref[h]) for h in range(MLA_H)]
    o_ref[...] = jnp.concatenate(pieces, axis=1)


def _mla_paged_kernel(pt_ref, *refs, n_new):
    pps = len(refs) - 7
    pages = refs[:pps]
    q_ref, new_ref, wuv_ref, o_ref, m_ref, l_ref, acc_ref = refs[pps:]
    c = pl.program_id(1)
    r = SAMPLE_ROWS
    q = q_ref[...].reshape(MLA_H * r, MLA_LAT)

    @pl.when(c == 0)
    def _():
        k = new_ref[...].astype(BF16)
        s = _heads3(_dot_nt(q, k), MLA_H)
        s = jnp.where(_new_row_mask(n_new)[None], s, NEG).reshape(MLA_H * r, r)
        _flash_init(s, k[:, :MLA_KV_RANK], m_ref, l_ref, acc_ref)

    kt = jnp.concatenate([p[0] for p in pages], axis=1).astype(BF16)
    _flash_step(_dot(q, kt), kt[:MLA_KV_RANK], m_ref, l_ref, acc_ref, v_feature_major=True)

    @pl.when(c == pl.num_programs(1) - 1)
    def _():
        _mla_finish(acc_ref[...], l_ref[...], wuv_ref, o_ref, r)


def _mla_paged(pt, cache, q_mla, lat_new, wuv, n_seq, n_pages, pps, n_new, name):
    r = SAMPLE_ROWS
    hr = MLA_H * r
    grid_spec = pltpu.PrefetchScalarGridSpec(
        num_scalar_prefetch=1,
        grid=(n_seq, n_pages // pps),
        in_specs=_page_specs(MLA_LAT, PAGE, n_pages, pps) + [
            pl.BlockSpec((MLA_H, r, MLA_LAT), lambda b, c, pt: (0, b, 0)),
            pl.BlockSpec((r, MLA_LAT), lambda b, c, pt: (b, 0)),
            pl.BlockSpec((MLA_H, MLA_KV_RANK, MLA_DV), lambda b, c, pt: (0, 0, 0))],
        out_specs=pl.BlockSpec((r, 512), lambda b, c, pt: (b, 0)),
        scratch_shapes=[pltpu.VMEM((hr, 1), F32), pltpu.VMEM((hr, 1), F32),
                        pltpu.VMEM((hr, MLA_KV_RANK), F32)],
    )
    return pl.pallas_call(
        functools.partial(_mla_paged_kernel, n_new=n_new),
        grid_spec=grid_spec,
        out_shape=jax.ShapeDtypeStruct((n_seq * r, 512), F32),
        compiler_params=_cparams(("parallel", "arbitrary")),
        name=name,
    )(pt, *([cache] * pps), q_mla, lat_new, wuv)


def _diff_paged_kernel(pt_ref, *refs, n_new, lam_init):
    pps = len(refs) - 10
    pages = refs[:pps]
    q_ref, new_ref, bl_ref, bn_ref, lp_ref, g_ref, o_ref, m_ref, l_ref, acc_ref = refs[pps:]
    c = pl.program_id(1)
    nc = pl.num_programs(1)
    r = SAMPLE_ROWS
    hr = 2 * DIFF_H * r
    q = _diff_queries(q_ref[...])

    def scores(k, bias, mask):
        s = _heads3(_dot_nt(q, k), 2 * DIFF_H)
        if bias is not None:
            s = s + jnp.concatenate([bias, bias], axis=0)
        if mask is not None:
            s = jnp.where(mask[None], s, NEG)
        return s.reshape(hr, k.shape[0])

    @pl.when(c == 0)
    def _():
        kv = new_ref[...].astype(BF16)
        _flash_init(scores(kv[:, :2 * DIFF_DH], bn_ref[...], _new_row_mask(n_new)), kv[:, 2 * DIFF_DH:],
                    m_ref, l_ref, acc_ref)

    k = jnp.concatenate([p[0, pl.ds(0, PAGE, stride=2), :] for p in pages], axis=0).astype(BF16)
    v = jnp.concatenate([p[0, pl.ds(1, PAGE, stride=2), :] for p in pages], axis=0).astype(BF16)

    @pl.when(c < nc - 1)
    def _():
        _flash_step(scores(k, None, None), v, m_ref, l_ref, acc_ref)

    @pl.when(c == nc - 1)
    def _():
        _flash_step(scores(k, bl_ref[...], None), v, m_ref, l_ref, acc_ref)
        _diff_finish(acc_ref[...], l_ref[...], lp_ref, g_ref, o_ref, r, lam_init)


def _diff_paged(pt, cache, us, bl, bn, lp, g, lam_init, n_seq, n_pages, pps, n_new, name):
    r = SAMPLE_ROWS
    hr = 2 * DIFF_H * r
    ck = pps * PAGE
    res = lambda shape: pl.BlockSpec(shape, lambda b, c, pt: (0,) * len(shape))
    grid_spec = pltpu.PrefetchScalarGridSpec(
        num_scalar_prefetch=1,
        grid=(n_seq, n_pages // pps),
        in_specs=_page_specs(2 * PAGE, 2 * DIFF_DH, n_pages, pps) + [
            pl.BlockSpec((r, 512), lambda b, c, pt: (b, _US_DIFF_Q // 512)),
            pl.BlockSpec((r, 4 * DIFF_DH), lambda b, c, pt: (b, _US_DIFF_KV // (4 * DIFF_DH))),
            res((DIFF_H, r, ck)), res((DIFF_H, r, r)), res((4, DIFF_DH)), res((1, 2 * DIFF_DH))],
        out_specs=pl.BlockSpec((r, 512), lambda b, c, pt: (b, 0)),
        scratch_shapes=[pltpu.VMEM((hr, 1), F32), pltpu.VMEM((hr, 1), F32),
                        pltpu.VMEM((hr, 2 * DIFF_DH), F32)],
    )
    return pl.pallas_call(
        functools.partial(_diff_paged_kernel, n_new=n_new, lam_init=lam_init),
        grid_spec=grid_spec,
        out_shape=jax.ShapeDtypeStruct((n_seq * r, 512), F32),
        compiler_params=_cparams(("parallel", "arbitrary")),
        name=name,
    )(pt, *([cache] * pps), us, us, bl, bn, lp, g)


def _layer_weights(l, w_in, pre_norm_g, post_norm_g, mla_q_norm_g, w_mla_uq, mla_kv_norm_g, w_mla_uk,
                   w_mla_uv, lam_params, diff_subln_g, mem_norm_g, w_mem_kv, w_branch, w_out):
    w = w_in[l]
    widths = [NSA_H * NSA_DH, 128, 128, 128, 3 * NSA_H, MLA_Q_RANK, MLA_KV_RANK, MLA_ROPE,
              DIFF_H * 2 * DIFF_DH, 4 * DIFF_DH, MEM_H * MEM_DH, N_BRANCH * BRANCH_W, N_BRANCH * D_MODEL]
    offs = np.concatenate([[0], np.cumsum(widths)])
    (nsa_q, cmp_, slc, win, gate, cq, ckv, kr, diff_q, diff_kv, mem_q, z, mg) = [
        w[:, offs[i]:offs[i + 1]] for i in range(len(widths))]
    half = MLA_ROPE // 2
    gate_t = gate.reshape(D_MODEL, NSA_H, 3).transpose(0, 2, 1).reshape(D_MODEL, 3 * NSA_H)
    kr_sw = jnp.concatenate([kr[:, half:], kr[:, :half]], axis=1)
    misc = jnp.concatenate([gate_t, kr, kr_sw, jnp.zeros((D_MODEL, 128 - 3 * NSA_H - 2 * MLA_ROPE), F32)], axis=1)
    w_small = jnp.concatenate([nsa_q, diff_q, mem_q, cq, diff_kv, cmp_, slc, win, ckv, misc], axis=1)
    w_big = jnp.concatenate([mg, z], axis=1)
    uq = w_mla_uq[l].reshape(MLA_Q_RANK, MLA_H, MLA_NOPE + MLA_ROPE)
    uq_r = uq[:, :, MLA_NOPE:]
    uq_rs = jnp.concatenate([uq_r[:, :, half:], uq_r[:, :, :half]], axis=2)
    return {
        'pre_g': pre_norm_g[l], 'post_g': post_norm_g[l].reshape(1, D_MODEL),
        'w_small': w_small.astype(BF16), 'w_big': w_big.astype(BF16),
        'gq': mla_q_norm_g[l].reshape(1, MLA_Q_RANK), 'gkv': mla_kv_norm_g[l].reshape(1, MLA_KV_RANK),
        'w_uq_n': uq[:, :, :MLA_NOPE].reshape(MLA_Q_RANK, MLA_H * MLA_NOPE).astype(BF16),
        'w_uq_r': uq_r.reshape(MLA_Q_RANK, MLA_H * MLA_ROPE).astype(BF16),
        'w_uq_rs': uq_rs.reshape(MLA_Q_RANK, MLA_H * MLA_ROPE).astype(BF16),
        'w_ukT': w_mla_uk[l].reshape(MLA_KV_RANK, MLA_H, MLA_NOPE).transpose(1, 2, 0).astype(BF16),
        'w_uv': w_mla_uv[l].reshape(MLA_KV_RANK, MLA_H, MLA_DV).transpose(1, 0, 2).astype(BF16),
        'lam_params': lam_params[l], 'subln_g': diff_subln_g[l].reshape(1, 2 * DIFF_DH),
        'mem_g': mem_norm_g[l], 'w_mem_kv': w_mem_kv[l].astype(BF16),
        'w_branch': w_branch[l].astype(BF16), 'w_out': w_out[l].astype(BF16),
    }


def _rope_tables(pos):
    half = MLA_ROPE // 2
    inv = ROPE_THETA ** (-jnp.arange(half, dtype=F32) / half)
    ang = pos.astype(F32)[:, None] * inv[None, :]
    cos, sin = jnp.cos(ang), jnp.sin(ang)
    return jnp.concatenate([cos, cos], axis=1), jnp.concatenate([-sin, sin], axis=1)


def _in_proj(x, lw, tm, tag):
    ub = _rms_matmul(x, lw['pre_g'], lw['w_big'], BF16, tm, 2048, tag + '_proj_big')
    us = _rms_matmul(x, lw['pre_g'], lw['w_small'], F32, tm, _US_W, tag + '_proj_small')
    return ub, us


def kernel(x_prompt, x_sample, cache_nsa_cmp_kv, cache_nsa_slc_kv, cache_nsa_win_kv, cache_mla_latent, cache_diff_kv, cache_mem_kv, page_table, mem_prompt, rel_bias, pre_norm_g, post_norm_g, w_in, mla_q_norm_g, w_mla_uq, mla_kv_norm_g, w_mla_uk, w_mla_uv, diff_lambda_q1, diff_lambda_k1, diff_lambda_q2, diff_lambda_k2, diff_subln_g, mem_norm_g, w_mem_kv, w_branch, w_out):
    depth = w_in.shape[0]
    bp, t, _ = x_prompt.shape
    assert bp == 1, "prompt group is a single sequence"
    ns, n_new, _ = x_sample.shape
    n_pages = page_table.shape[1]
    past = n_pages * PAGE
    n_pool = cache_nsa_cmp_kv.shape[1]
    wbuf = cache_nsa_win_kv.shape[2]
    assert wbuf == NSA_WINDOW and n_new <= SAMPLE_ROWS and past % NSA_BLOCK == 0
    assert t % QT == 0 and NSA_WINDOW == 2 * TS
    pps = min(PAGES_PER_STEP, n_pages)
    assert n_pages % pps == 0
    ck = pps * PAGE
    r = SAMPLE_ROWS
    nb_p = t // NSA_BLOCK
    nb_s = past // NSA_BLOCK
    nbk_s = nb_s + 128
    bpt = TS // NSA_BLOCK

    lam_params = jnp.stack([diff_lambda_q1, diff_lambda_k1, diff_lambda_q2, diff_lambda_k2], axis=1).astype(F32)

    pos_p = jnp.arange(t, dtype=jnp.int32)
    tok = jnp.arange(r, dtype=jnp.int32)
    pos_s = jnp.tile(past + tok, ns)
    cos_p, sin_p = _rope_tables(pos_p)
    cos_s, sin_s = _rope_tables(pos_s)
    ar = lambda n: jnp.arange(n, dtype=jnp.int32)
    d_tile = jnp.stack([ar(TS)[:, None] - ar(TS)[None, :], TS + ar(TS)[:, None] - ar(TS)[None, :]]).reshape(2 * TS, TS)
    d_cmp_p = pos_p[:, None] - (ar(nb_p)[None, :] * NSA_BLOCK + NSA_BLOCK - 1)
    d_cmp_s = (past + tok)[:, None] - (ar(nbk_s)[None, :] * NSA_BLOCK + NSA_BLOCK - 1)
    d_last = ck + tok[:, None] - ar(ck)[None, :]
    d_new = tok[:, None] - tok[None, :]
    d_win = wbuf + tok[:, None] - ar(wbuf)[None, :]
    bt = _bias_expand(rel_bias, d_tile, 0, BIAS_H, 'bias_tiles').reshape(BIAS_H, 2, TS, TS)
    bias_cmp_p = _bias_expand(rel_bias, d_cmp_p, 0, NSA_H, 'bias_cmp_prompt')
    bias_cmp_s = _bias_expand(rel_bias, d_cmp_s, 0, NSA_H, 'bias_cmp_sample')
    bias_last = _bias_expand(rel_bias, d_last, 0, BIAS_H, 'bias_last_chunk')
    bias_new = _bias_expand(rel_bias, d_new, 0, BIAS_H, 'bias_new_rows')
    bias_win = _bias_expand(rel_bias, d_win, 0, NSA_H, 'bias_window')
    ex8 = np.zeros((8, TS), np.float32)
    for cidx in range(bpt):
        ex8[cidx, cidx * NSA_BLOCK:(cidx + 1) * NSA_BLOCK] = 1.0
    ex8 = jnp.asarray(ex8, BF16)
    ex = jnp.asarray(np.repeat(np.eye(ck // NSA_BLOCK, dtype=np.float32), NSA_BLOCK, axis=1), BF16)

    pt_flat = page_table.reshape(-1).astype(jnp.int32)
    c_cmp = cache_nsa_cmp_kv.reshape(depth * n_pool, PAGE, 128)
    c_slc = cache_nsa_slc_kv.reshape(depth * n_pool, PAGE, 128)
    c_lat = jnp.swapaxes(cache_mla_latent, 2, 3).reshape(depth * n_pool, MLA_LAT, PAGE)
    c_dkv = cache_diff_kv.reshape(depth * n_pool, 2 * PAGE, 2 * DIFF_DH)
    c_win = cache_nsa_win_kv.reshape(depth * ns, wbuf, 128)

    xp = x_prompt.reshape(t, D_MODEL)
    xs = jnp.pad(x_sample, ((0, 0), (0, r - n_new), (0, 0))).reshape(ns * r, D_MODEL)
    p_st = [[] for _ in range(6)]
    s_st = [[] for _ in range(5)]
    for l in range(depth):
        lam_init = 0.8 - 0.6 * math.exp(-0.3 * l)
        lw = _layer_weights(l, w_in, pre_norm_g, post_norm_g, mla_q_norm_g, w_mla_uq, mla_kv_norm_g, w_mla_uk,
                            w_mla_uv, lam_params, diff_subln_g, mem_norm_g, w_mem_kv, w_branch, w_out)
        tag = 'l%d' % l
        ub, us = _in_proj(xp, lw, min(512, t), tag + '_p')
        q_mla, lat = _mla_prep(us, cos_p, sin_p, lw, min(512, t), True, tag + '_p_mla_prep')
        kvc = _pool_prompt(us, tag + '_p_pool')
        oc, cnt = _nsa_cmp(us, kvc, bias_cmp_p, pos_p.reshape(t, 1), nb_p, TS, False, tag + '_p_nsa_cmp')
        cmp_new = us[:, _US_CMP:_US_CMP + 128]
        slc_new = us[:, _US_SLC:_US_SLC + 128]
        win_new = us[:, _US_WIN:_US_WIN + 128]
        dkv_new = us[:, _US_DIFF_KV:_US_DIFF_KV + 4 * DIFF_DH]
        cnt8 = jnp.pad(cnt.reshape(t, nb_p // bpt, bpt).transpose(1, 2, 0), ((0, 0), (0, 8 - bpt), (0, 0)))
        slc_bf, win_bf = slc_new.astype(BF16), win_new.astype(BF16)
        o_nsa = _nsa_prompt(us, oc, cnt8, _key_tiles(slc_bf), _with_ones(slc_bf, 128), _key_tiles(win_bf), win_bf,
                            bt[:NSA_H], ex8, tag + '_p_nsa_attn')
        lat_bf = lat.astype(BF16)
        mla_k = jnp.concatenate([lat_bf[:, MLA_KV_RANK:], jnp.zeros((t, MLA_QW - MLA_LAT), BF16),
                                 lat_bf[:, :MLA_KV_RANK]], axis=1)
        o_mla = _mla_prompt(q_mla, _key_tiles(mla_k), _with_ones(lat_bf[:, :MLA_KV_RANK], MLA_KV_RANK), lw['w_uv'],
                            tag + '_p_mla_attn')
        dkv_bf = dkv_new.astype(BF16)
        o_diff = _diff_prompt(us, _key_tiles(dkv_bf[:, :2 * DIFF_DH]), _with_ones(dkv_bf[:, 2 * DIFF_DH:], 2 * DIFF_DH),
                              bt[NSA_H:], lw['lam_params'], lw['subln_g'], lam_init, tag + '_p_diff_attn')
        mkv = _rms_matmul(mem_prompt.reshape(N_MEM, D_MODEL), lw['mem_g'], lw['w_mem_kv'], F32,
                          N_MEM, MEM_H * 2 * MEM_DH, tag + '_p_mem_kv')
        o_mem = _mem_attend(us, mkv.reshape(1, N_MEM, -1), TS, False, tag + '_p_mem_attn')
        xp = _merge(xp, [o_nsa, o_mla, o_diff, o_mem], ub, lw, TS, tag + '_p_merge')
        win_keep = min(NSA_WINDOW, t)
        for i, a in enumerate((cmp_new.reshape(1, t, 1, 128), slc_new.reshape(1, t, 1, 128),
                               win_new[t - win_keep:].reshape(1, win_keep, 1, 128), lat.reshape(1, t, MLA_LAT),
                               dkv_new.reshape(1, t, 1, 4 * DIFF_DH), mkv.reshape(1, N_MEM, MEM_H, 2 * MEM_DH))):
            p_st[i].append(a)
        pt_l = pt_flat + l * n_pool
        ub, us = _in_proj(xs, lw, min(512, ns * r), tag + '_s')
        q_mla, lat = _mla_prep(us, cos_s, sin_s, lw, min(512, ns * r), False, tag + '_s_mla_prep')
        kvc_past, kvc_part = _pool_paged(pt_l, c_cmp, us, ns, n_pages, pps, tag + '_s_pool')
        kvc = jnp.concatenate([kvc_past, kvc_part], axis=1)
        oc, cnt = _nsa_cmp(us, kvc, bias_cmp_s, pos_s.reshape(ns * r, 1), nb_s + 1, r, True, tag + '_s_nsa_cmp')
        cnt4 = cnt[:, :nb_s].reshape(ns, r, n_pages // pps, ck // NSA_BLOCK).transpose(0, 2, 1, 3)
        o_nsa = _nsa_paged(pt_l, c_slc, us, oc, cnt4, cnt, c_win, l * ns, bias_last[:NSA_H], bias_new[:NSA_H],
                           bias_win, ex, ns, n_pages, pps, n_new, tag + '_s_nsa_attn')
        o_mla = _mla_paged(pt_l, c_lat, q_mla, lat, lw['w_uv'], ns, n_pages, pps, n_new, tag + '_s_mla_attn')
        o_diff = _diff_paged(pt_l, c_dkv, us, bias_last[NSA_H:], bias_new[NSA_H:], lw['lam_params'], lw['subln_g'],
                             lam_init, ns, n_pages, pps, n_new, tag + '_s_diff_attn')
        c_mem = cache_mem_kv[l].reshape(ns, N_MEM, MEM_H * 2 * MEM_DH)
        o_mem = _mem_attend(us, c_mem, r, True, tag + '_s_mem_attn')
        xs = _merge(xs, [o_nsa, o_mla, o_diff, o_mem], ub, lw, min(256, ns * r), tag + '_s_merge')
        us3 = us.reshape(ns, r, _US_W)[:, :n_new]
        new_win = us3[:, :, _US_WIN:_US_WIN + 128]
        s_win = jnp.concatenate([c_win[l * ns:(l + 1) * ns, n_new:], new_win], axis=1)
        for i, a in enumerate((us3[:, :, _US_CMP:_US_CMP + 128].reshape(ns, n_new, 1, 128),
                               us3[:, :, _US_SLC:_US_SLC + 128].reshape(ns, n_new, 1, 128),
                               s_win.reshape(ns, wbuf, 1, 128),
                               lat.reshape(ns, r, MLA_LAT)[:, :n_new],
                               us3[:, :, _US_DIFF_KV:_US_DIFF_KV + 4 * DIFF_DH].reshape(ns, n_new, 1, 4 * DIFF_DH))):
            s_st[i].append(a)
    p_nsa_cmp, p_nsa_slc, p_nsa_win, p_mla_latent, p_diff_kv, p_mem_kv = [jnp.stack(a, axis=0) for a in p_st]
    s_nsa_cmp, s_nsa_slc, s_nsa_win, s_mla_latent, s_diff_kv = [jnp.stack(a, axis=0) for a in s_st]
    y_prompt = xp.reshape(1, t, D_MODEL)
    y_sample = xs.reshape(ns, r, D_MODEL)[:, :n_new]
    return (y_prompt, y_sample, p_nsa_cmp, s_nsa_cmp, p_nsa_slc, s_nsa_slc, p_nsa_win, s_nsa_win,
            p_mla_latent, s_mla_latent, p_diff_kv, s_diff_kv, p_mem_kv)
```

```python
import functools
import math

import numpy as np
import jax
import jax.numpy as jnp
from jax import lax
from jax.experimental import pallas as pl
from jax.experimental.pallas import tpu as pltpu

F32 = jnp.float32
BF16 = jnp.bfloat16

D_MODEL = 1024
PAGE = 128
BRANCH_W = 512
N_BRANCH = 4
NSA_H = 8
NSA_DH = 64
NSA_BLOCK = 64
NSA_TOPN = 16
NSA_WINDOW = 512
MLA_H = 8
MLA_Q_RANK = 256
MLA_KV_RANK = 128
MLA_NOPE = 64
MLA_ROPE = 32
MLA_DV = 64
MLA_LAT = MLA_KV_RANK + MLA_ROPE
ROPE_THETA = 10000.0
DIFF_H = 4
DIFF_DH = 64
N_MEM = 256
MEM_H = 4
MEM_DH = 128
N_BUCKETS = 32
MAX_DISTANCE = 128
BIAS_H = NSA_H + DIFF_H
EPS = 1e-6
NEG = -1e30
TINY = float(np.finfo(np.float32).tiny)

VMEM_LIMIT = 56 * 1024 * 1024

SAMPLE_ROWS = 8
PAGES_PER_STEP = 64
SEQS_PER_CMP_STEP = 16
TS = 256
QT = 2 * TS
MLA_QW = 256

_US_NSA_Q, _US_DIFF_Q, _US_MEM_Q, _US_CQ, _US_DIFF_KV = 0, 512, 1024, 1536, 1792
_US_CMP, _US_SLC, _US_WIN, _US_CKV, _US_MISC, _US_W = 2048, 2176, 2304, 2432, 2560, 2688
_MISC_KR, _MISC_KRS = 24, 56


def _bucket_thresholds():
    n = np.arange(0, 4 * MAX_DISTANCE, dtype=np.int64)
    exact = N_BUCKETS // 2
    nf = np.maximum(n, exact).astype(np.float32)
    large = exact + (np.log(nf / np.float32(exact)) / np.float32(math.log(MAX_DISTANCE / exact))
                     * np.float32(N_BUCKETS - exact)).astype(np.int32)
    bucket = np.where(n < exact, n, np.minimum(large, N_BUCKETS - 1))
    return [int(np.argmax(bucket >= b)) for b in range(N_BUCKETS)]


_THR = _bucket_thresholds()


def _cparams(sem):
    return pltpu.CompilerParams(dimension_semantics=sem, vmem_limit_bytes=VMEM_LIMIT)


def _resident(shape):
    nd = len(shape)
    return pl.BlockSpec(shape, lambda *a: (0,) * nd, pipeline_mode=pl.Buffered(1))


def _dot(a, b):
    return jnp.dot(a, b, preferred_element_type=F32)


def _dot_nt(a, b):
    return lax.dot_general(a, b, (((1,), (1,)), ((), ())), preferred_element_type=F32)


def _dot_tn(a, b):
    return lax.dot_general(a, b, (((0,), (0,)), ((), ())), preferred_element_type=F32)


def _rms_rows(x, g):
    return x * lax.rsqrt(jnp.mean(x * x, axis=-1, keepdims=True) + EPS) * g


def _rms_matmul_kernel(x_ref, g_ref, w_ref, o_ref):
    y = _rms_rows(x_ref[...], g_ref[...])
    o_ref[...] = _dot(y.astype(BF16), w_ref[...]).astype(o_ref.dtype)


def _rms_matmul(x, g, w, out_dtype, tm, tn, name):
    m, k = x.shape
    n = w.shape[1]
    return pl.pallas_call(
        _rms_matmul_kernel,
        grid=(n // tn, m // tm),
        in_specs=[pl.BlockSpec((tm, k), lambda j, i: (i, 0)),
                  pl.BlockSpec((1, k), lambda j, i: (0, 0)),
                  pl.BlockSpec((k, tn), lambda j, i: (0, j))],
        out_specs=pl.BlockSpec((tm, tn), lambda j, i: (i, j)),
        out_shape=jax.ShapeDtypeStruct((m, n), out_dtype),
        compiler_params=_cparams(("parallel", "parallel")),
        name=name,
    )(x, g.reshape(1, k), w)


def _bias_expand_kernel(tbl_ref, d_ref, o_ref, *, h0):
    h = pl.program_id(0) + h0
    d = d_ref[...]
    out = jnp.full(d.shape, tbl_ref[0, h], F32)
    for b in range(1, N_BUCKETS):
        out = jnp.where(d >= _THR[b], tbl_ref[b, h], out)
    o_ref[0] = out - tbl_ref[N_BUCKETS - 1, h]


def _bias_expand(tbl, dist, h0, nh, name):
    r, c = dist.shape
    tr = min(r, 512)
    return pl.pallas_call(
        functools.partial(_bias_expand_kernel, h0=h0),
        grid=(nh, r // tr),
        in_specs=[pl.BlockSpec(memory_space=pltpu.SMEM),
                  pl.BlockSpec((tr, c), lambda h, i: (i, 0))],
        out_specs=pl.BlockSpec((1, tr, c), lambda h, i: (h, i, 0)),
        out_shape=jax.ShapeDtypeStruct((nh, r, c), F32),
        compiler_params=_cparams(("parallel", "parallel")),
        name=name,
    )(tbl, dist)


def _mla_prep_kernel(cq_ref, ckv_ref, misc_ref, cos_ref, sin_ref, gq_ref, gkv_ref,
                     wn_ref, wr_ref, wrs_ref, wuk_ref, q_ref, lat_ref, *, rope_first):
    scale = (MLA_NOPE + MLA_ROPE) ** -0.5
    cqn = _rms_rows(cq_ref[...], gq_ref[...]).astype(BF16)
    qn = _dot(cqn, wn_ref[...])
    qr = _dot(cqn, wr_ref[...])
    qrs = _dot(cqn, wrs_ref[...])
    cos = cos_ref[...]
    sin = sin_ref[...]
    qw = q_ref.shape[-1]
    lat0, rope0 = (qw - MLA_KV_RANK, 0) if rope_first else (0, MLA_KV_RANK)
    if rope_first:
        q_ref[...] = jnp.zeros(q_ref.shape, BF16)
    for h in range(MLA_H):
        ql = _dot(qn[:, h * MLA_NOPE:(h + 1) * MLA_NOPE].astype(BF16), wuk_ref[h])
        q_ref[h, :, lat0:lat0 + MLA_KV_RANK] = (ql * scale).astype(BF16)
        sl = slice(h * MLA_ROPE, (h + 1) * MLA_ROPE)
        rope = qr[:, sl] * cos + qrs[:, sl] * sin
        q_ref[h, :, rope0:rope0 + MLA_ROPE] = (rope * scale).astype(BF16)
    lat_ref[:, 0:MLA_KV_RANK] = _rms_rows(ckv_ref[...], gkv_ref[...])
    misc = misc_ref[...]
    kr = misc[:, _MISC_KR:_MISC_KR + MLA_ROPE]
    krs = misc[:, _MISC_KRS:_MISC_KRS + MLA_ROPE]
    lat_ref[:, MLA_KV_RANK:MLA_LAT] = kr * cos + krs * sin


def _mla_prep(us, cos, sin, lw, tm, rope_first, name):
    m = us.shape[0]
    qw = MLA_QW if rope_first else MLA_LAT
    col = lambda off, w: pl.BlockSpec((tm, w), lambda i: (i, off // w))
    return pl.pallas_call(
        functools.partial(_mla_prep_kernel, rope_first=rope_first),
        grid=(m // tm,),
        in_specs=[col(_US_CQ, MLA_Q_RANK), col(_US_CKV, MLA_KV_RANK), col(_US_MISC, 128),
                  pl.BlockSpec((tm, MLA_ROPE), lambda i: (i, 0)),
                  pl.BlockSpec((tm, MLA_ROPE), lambda i: (i, 0)),
                  _resident((1, MLA_Q_RANK)), _resident((1, MLA_KV_RANK)),
                  _resident((MLA_Q_RANK, MLA_H * MLA_NOPE)),
                  _resident((MLA_Q_RANK, MLA_H * MLA_ROPE)),
                  _resident((MLA_Q_RANK, MLA_H * MLA_ROPE)),
                  _resident((MLA_H, MLA_NOPE, MLA_KV_RANK))],
        out_specs=[pl.BlockSpec((MLA_H, tm, qw), lambda i: (0, i, 0)),
                   pl.BlockSpec((tm, MLA_LAT), lambda i: (i, 0))],
        out_shape=[jax.ShapeDtypeStruct((MLA_H, m, qw), BF16),
                   jax.ShapeDtypeStruct((m, MLA_LAT), F32)],
        compiler_params=_cparams(("parallel",)),
        name=name,
    )(us, us, us, cos, sin, lw['gq'], lw['gkv'], lw['w_uq_n'], lw['w_uq_r'], lw['w_uq_rs'], lw['w_ukT'])


def _pad_heads_128(q, n_heads, scale):
    r = q.shape[0]
    lane = lax.broadcasted_iota(jnp.int32, (r, 128), 1)
    parts = []
    for h in range(n_heads):
        slab = q[:, (h // 2) * 128:(h // 2 + 1) * 128]
        if h % 2 == 1:
            slab = pltpu.roll(slab, 64, 1)
        parts.append(jnp.where(lane < 64, slab * scale, 0.0).astype(BF16))
    return jnp.concatenate(parts, axis=0)


def _flash_init(s, v, m_ref, l_ref, acc_ref, w=None):
    m = jnp.max(s, axis=-1, keepdims=True)
    p = jnp.exp(s - m)
    if w is not None:
        p = p * w
    m_ref[...] = m
    l_ref[...] = jnp.sum(p, axis=-1, keepdims=True)
    acc_ref[...] = _dot(p.astype(BF16), v)


def _flash_step(s, v, m_ref, l_ref, acc_ref, w=None, v_feature_major=False):
    m_prev = m_ref[...]
    m_new = jnp.maximum(m_prev, jnp.max(s, axis=-1, keepdims=True))
    a = jnp.exp(m_prev - m_new)
    p = jnp.exp(s - m_new)
    if w is not None:
        p = p * w
    l_ref[...] = a * l_ref[...] + jnp.sum(p, axis=-1, keepdims=True)
    pv = _dot_nt(p.astype(BF16), v) if v_feature_major else _dot(p.astype(BF16), v)
    acc_ref[...] = a * acc_ref[...] + pv
    m_ref[...] = m_new


def _heads3(x, h):
    return x.reshape(h, x.shape[0] // h, x.shape[1])


def _sigmoid(x):
    return 1.0 / (1.0 + jnp.exp(-x))


def _lane_reduce(x, op, reduce):
    w = x.shape[1]
    if w % 128 == 0 and w > 128:
        slab = x[:, 0:128]
        for i in range(1, w // 128):
            slab = op(slab, x[:, i * 128:(i + 1) * 128])
        x = slab
    return reduce(x, axis=-1, keepdims=True)


def _nsa_cmp_kernel(q_ref, misc_ref, kvc_ref, bias_ref, qpos_ref, oc_ref, cnt_ref, *, nb_true):
    n_grp = kvc_ref.shape[0]
    rows = q_ref.shape[0]
    r = rows // n_grp
    nbk = kvc_ref.shape[-2]
    qpos = qpos_ref[...]
    blk = lax.broadcasted_iota(jnp.int32, (rows, nbk), 1)
    valid = jnp.logical_and(blk * NSA_BLOCK + (NSA_BLOCK - 1) <= qpos, blk < nb_true)
    g0 = _sigmoid(misc_ref[:, 0:NSA_H])
    works = []
    for g in range(n_grp):
        rs = slice(g * r, (g + 1) * r)
        kv = kvc_ref[g].astype(BF16)
        qp = _pad_heads_128(q_ref[rs, :], NSA_H, NSA_DH ** -0.5)
        vg = valid[rs]
        s = _heads3(_dot_nt(qp, kv), NSA_H) + bias_ref[...]
        s = jnp.where(vg[None], s, NEG)
        m = jnp.max(s, axis=-1, keepdims=True)
        e = jnp.where(vg[None], jnp.exp(s - m), 0.0)
        p = e / jnp.maximum(jnp.sum(e, axis=-1, keepdims=True), TINY)
        o = _dot(p.reshape(NSA_H * r, nbk).astype(BF16), kv)
        pieces = [o[h * r:(h + 1) * r, NSA_DH:] * g0[rs, h:h + 1] for h in range(NSA_H)]
        oc_ref[rs, :] = jnp.concatenate(pieces, axis=1)
        works.append(jnp.where(vg, jnp.sum(p, axis=0), -1.0))
    work = works[0] if n_grp == 1 else jnp.concatenate(works, axis=0)
    cur = lax.shift_right_arithmetic(qpos, int(math.log2(NSA_BLOCK)))
    cnt = jnp.where(blk == cur, 1.0, 0.0)
    blkf = blk.astype(F32)
    for _ in range(min(NSA_TOPN, nb_true)):
        mx = _lane_reduce(work, jnp.maximum, jnp.max)
        idx = _lane_reduce(jnp.where(work == mx, blkf, float(nbk)), jnp.minimum, jnp.min)
        sel = blkf == idx
        cnt = cnt + jnp.where(jnp.logical_and(sel, mx >= 0.0), 1.0, 0.0)
        work = jnp.where(sel, -2.0, work)
    cnt_ref[...] = cnt.astype(BF16)


def _nsa_cmp(us, kvc, bias_c, qpos, nb_true, rows, seqs_per_step, name):
    m = us.shape[0]
    nbk = kvc.shape[-2]
    col = lambda off, w: pl.BlockSpec((rows, w), lambda i: (i, off // w))
    if seqs_per_step:
        kvc_spec = pl.BlockSpec((seqs_per_step, nbk, 128), lambda i: (i, 0, 0))
        bias_spec = _resident((NSA_H, rows // seqs_per_step, nbk))
    else:
        kvc_spec = _resident((1, nbk, 128))
        bias_spec = pl.BlockSpec((NSA_H, rows, nbk), lambda i: (0, i, 0))
    return pl.pallas_call(
        functools.partial(_nsa_cmp_kernel, nb_true=nb_true),
        grid=(m // rows,),
        in_specs=[col(_US_NSA_Q, 512), col(_US_MISC, 128), kvc_spec, bias_spec,
                  pl.BlockSpec((rows, 1), lambda i: (i, 0))],
        out_specs=[pl.BlockSpec((rows, 512), lambda i: (i, 0)),
                   pl.BlockSpec((rows, nbk), lambda i: (i, 0))],
        out_shape=[jax.ShapeDtypeStruct((m, 512), F32), jax.ShapeDtypeStruct((m, nbk), BF16)],
        compiler_params=_cparams(("parallel",)),
        name=name,
    )(us, us, kvc, bias_c, qpos)


def _pool_kernel(x_ref, o_ref):
    x = x_ref[...]
    nb = x.shape[0] // NSA_BLOCK
    o_ref[0] = jnp.sum(x.reshape(nb, NSA_BLOCK, 128), axis=1) * (1.0 / NSA_BLOCK)


def _pool_prompt(us, name):
    t = us.shape[0]
    rows = min(t, 2048)
    nb = t // NSA_BLOCK
    return pl.pallas_call(
        _pool_kernel,
        grid=(t // rows,),
        in_specs=[pl.BlockSpec((rows, 128), lambda i: (i, _US_CMP // 128))],
        out_specs=pl.BlockSpec((1, rows // NSA_BLOCK, 128), lambda i: (0, i, 0)),
        out_shape=jax.ShapeDtypeStruct((1, nb, 128), F32),
        compiler_params=_cparams(("parallel",)),
        name=name,
    )(us)


def _rowmax(tiles):
    m = None
    for s in tiles:
        for i in range(s.shape[1] // 128):
            c = s[:, i * 128:(i + 1) * 128]
            m = c if m is None else jnp.maximum(m, c)
    return jnp.max(m, axis=-1, keepdims=True)


def _lane_tile(x, width):
    n = width // x.shape[1]
    return x if n == 1 else jnp.concatenate([x] * n, axis=1)


def _flash_rows(s_tiles, v, m_ref, acc_ref, idx, w_tiles=None):
    m_prev = m_ref[idx]
    m_new = jnp.maximum(m_prev, _rowmax(s_tiles))
    a = jnp.exp(m_prev - m_new)
    ps = []
    for i, s in enumerate(s_tiles):
        p = jnp.exp(s - _lane_tile(m_new, s.shape[1]))
        if w_tiles is not None:
            p = p * w_tiles[i]
        ps.append(p.astype(BF16))
    p = ps[0] if len(ps) == 1 else jnp.concatenate(ps, axis=1)
    acc = acc_ref[idx]
    acc_ref[idx] = _lane_tile(a, acc.shape[1]) * acc + _dot(p, v)
    m_ref[idx] = m_new


def _causal_tile():
    row = lax.broadcasted_iota(jnp.int32, (TS, TS), 0)
    colk = lax.broadcasted_iota(jnp.int32, (TS, TS), 1)
    return row, colk


def _rows(ref, start, size):
    return ref[pl.ds(pl.multiple_of(start, TS), size), :]


def _far_steps(n_pairs, step):
    def body(c, carry):
        step(2 * c, 2)
        return carry

    lax.fori_loop(0, n_pairs, body, 0)


def _mla_prompt_kernel(q_ref, kt_ref, v_ref, wuv_ref, o_ref, m_ref, acc_ref):
    qi = pl.program_id(0)
    m_ref[...] = jnp.full(m_ref.shape, NEG, F32)
    acc_ref[...] = jnp.zeros(acc_ref.shape, F32)

    def qblk(u, h):
        return q_ref[h, u * TS:(u + 1) * TS, :]

    def step(tile0, ntiles):
        kts = [kt_ref[tile0 + i] for i in range(ntiles)]
        v = _rows(v_ref, tile0 * TS, ntiles * TS)
        for u in range(2):
            for h in range(MLA_H):
                q = qblk(u, h)
                _flash_rows([_dot(q, kt) for kt in kts], v, m_ref, acc_ref, u * MLA_H + h)

    _far_steps(qi, step)

    row, colk = _causal_tile()
    causal = colk <= row
    kt_a, kt_b = kt_ref[2 * qi], kt_ref[2 * qi + 1]
    v_ab = _rows(v_ref, qi * QT, QT)
    for h in range(MLA_H):
        s = jnp.where(causal, _dot(qblk(0, h), kt_a), NEG)
        _flash_rows([s], v_ab[:TS], m_ref, acc_ref, h)
    for h in range(MLA_H):
        q = qblk(1, h)
        s1 = jnp.where(causal, _dot(q, kt_b), NEG)
        _flash_rows([_dot(q, kt_a), s1], v_ab, m_ref, acc_ref, MLA_H + h)

    for u in range(2):
        pieces = []
        for h in range(MLA_H):
            acc = acc_ref[u * MLA_H + h]
            o_lat = acc[:, :MLA_KV_RANK] / acc[:, MLA_KV_RANK:]
            pieces.append(_dot(o_lat.astype(BF16), wuv_ref[h]))
        o_ref[u * TS:(u + 1) * TS, :] = jnp.concatenate(pieces, axis=1)


def _mla_prompt(q_mla, kt, v_ext, wuv, name):
    t = v_ext.shape[0]
    nblk = 2 * MLA_H
    return pl.pallas_call(
        _mla_prompt_kernel,
        grid=(t // QT,),
        in_specs=[pl.BlockSpec((MLA_H, QT, MLA_QW), lambda i: (0, i, 0)),
                  _resident(kt.shape), _resident(v_ext.shape), _resident((MLA_H, MLA_KV_RANK, MLA_DV))],
        out_specs=pl.BlockSpec((QT, 512), lambda i: (i, 0)),
        out_shape=jax.ShapeDtypeStruct((t, 512), F32),
        scratch_shapes=[pltpu.VMEM((nblk, TS, 128), F32), pltpu.VMEM((nblk, TS, 2 * MLA_KV_RANK), F32)],
        compiler_params=_cparams(("parallel",)),
        name=name,
    )(q_mla, kt, v_ext, wuv)


def _diff_queries(q):
    r = q.shape[0]
    lane = lax.broadcasted_iota(jnp.int32, (r, 128), 1)
    qs = q * (DIFF_DH ** -0.5)
    m1 = [jnp.where(lane < DIFF_DH, qs[:, h * 128:(h + 1) * 128], 0.0).astype(BF16) for h in range(DIFF_H)]
    m2 = [jnp.where(lane >= DIFF_DH, qs[:, h * 128:(h + 1) * 128], 0.0).astype(BF16) for h in range(DIFF_H)]
    return jnp.concatenate(m1 + m2, axis=0)


def _diff_lambda(lp_ref, lam_init):
    lp = lp_ref[...]
    return (jnp.exp(jnp.sum(lp[0:1] * lp[1:2], axis=-1, keepdims=True))
            - jnp.exp(jnp.sum(lp[2:3] * lp[3:4], axis=-1, keepdims=True)) + lam_init)


def _diff_combine(o1, o2, lam, g, lam_init):
    return _rms_rows(o1 - lam * o2, g) * (1.0 - lam_init)


def _diff_finish(acc, l, lp_ref, g_ref, o_ref, rows, lam_init):
    lam = _diff_lambda(lp_ref, lam_init)
    o = acc / l
    half = DIFF_H * rows
    y = _diff_combine(o[:half], o[half:], lam, g_ref[...], lam_init)
    o_ref[...] = jnp.concatenate([y[h * rows:(h + 1) * rows] for h in range(DIFF_H)], axis=1)


def _stage_queries(qs_ref, make, q_ref, nblk):
    for u in range(2):
        q = make(q_ref[u * TS:(u + 1) * TS, :])
        for r in range(nblk):
            qs_ref[u * nblk + r] = q[r * TS:(r + 1) * TS]


def _diff_prompt_kernel(q_ref, kt_ref, v_ref, bt_ref, lp_ref, g_ref, o_ref, qs_ref, m_ref, acc_ref, *, lam_init):
    qi = pl.program_id(0)
    nblk = 2 * DIFF_H
    m_ref[...] = jnp.full(m_ref.shape, NEG, F32)
    acc_ref[...] = jnp.zeros(acc_ref.shape, F32)
    _stage_queries(qs_ref, _diff_queries, q_ref, nblk)

    def step(c, near_for_a):
        kts = [kt_ref[2 * c], kt_ref[2 * c + 1]]
        v = _rows(v_ref, c * QT, QT)
        for u in range(2):
            for r in range(nblk):
                q = qs_ref[u * nblk + r]
                s0, s1 = _dot(q, kts[0]), _dot(q, kts[1])
                if near_for_a and u == 0:
                    s1 = s1 + bt_ref[r % DIFF_H, 1]
                _flash_rows([s0, s1], v, m_ref, acc_ref, u * nblk + r)

    def far(c, carry):
        step(c, False)
        return carry

    lax.fori_loop(0, jnp.maximum(qi - 1, 0), far, 0)

    @pl.when(qi >= 1)
    def _():
        step(qi - 1, True)

    row, colk = _causal_tile()
    causal = colk <= row
    kt_a, kt_b = kt_ref[2 * qi], kt_ref[2 * qi + 1]
    v_ab = _rows(v_ref, qi * QT, QT)
    for r in range(nblk):
        s = jnp.where(causal, _dot(qs_ref[r], kt_a) + bt_ref[r % DIFF_H, 0], NEG)
        _flash_rows([s], v_ab[:TS], m_ref, acc_ref, r)
    for r in range(nblk):
        q = qs_ref[nblk + r]
        s0 = _dot(q, kt_a) + bt_ref[r % DIFF_H, 1]
        s1 = jnp.where(causal, _dot(q, kt_b) + bt_ref[r % DIFF_H, 0], NEG)
        _flash_rows([s0, s1], v_ab, m_ref, acc_ref, nblk + r)

    lam = _diff_lambda(lp_ref, lam_init)
    g = g_ref[...]
    for u in range(2):
        pieces = []
        for h in range(DIFF_H):
            a1 = acc_ref[u * nblk + h]
            a2 = acc_ref[u * nblk + DIFF_H + h]
            o1 = a1[:, :2 * DIFF_DH] / a1[:, 2 * DIFF_DH:]
            o2 = a2[:, :2 * DIFF_DH] / a2[:, 2 * DIFF_DH:]
            pieces.append(_diff_combine(o1, o2, lam, g, lam_init))
        o_ref[u * TS:(u + 1) * TS, :] = jnp.concatenate(pieces, axis=1)


def _diff_prompt(us, kt, v_ext, bt, lp, g, lam_init, name):
    t = us.shape[0]
    nblk = 2 * 2 * DIFF_H
    return pl.pallas_call(
        functools.partial(_diff_prompt_kernel, lam_init=lam_init),
        grid=(t // QT,),
        in_specs=[pl.BlockSpec((QT, 512), lambda i: (i, _US_DIFF_Q // 512)),
                  _resident(kt.shape), _resident(v_ext.shape), _resident((DIFF_H, 2, TS, TS)),
                  _resident((4, DIFF_DH)), _resident((1, 2 * DIFF_DH))],
        out_specs=pl.BlockSpec((QT, 512), lambda i: (i, 0)),
        out_shape=jax.ShapeDtypeStruct((t, 512), F32),
        scratch_shapes=[pltpu.VMEM((nblk, TS, 128), BF16), pltpu.VMEM((nblk, TS, 128), F32),
                        pltpu.VMEM((nblk, TS, 4 * DIFF_DH), F32)],
        compiler_params=_cparams(("parallel",)),
        name=name,
    )(us, kt, v_ext, bt, lp, g)


def _nsa_prompt_kernel(q_ref, misc_ref, oc_ref, cnt_ref, kt_ref, v_ref, wkt_ref, wv_ref, bt_ref, ex_ref, o_ref,
                       qs_ref, m_ref, acc_ref):
    qi = pl.program_id(0)
    nblk = NSA_H
    m_ref[...] = jnp.full(m_ref.shape, NEG, F32)
    acc_ref[...] = jnp.zeros(acc_ref.shape, F32)
    _stage_queries(qs_ref, lambda q: _pad_heads_128(q, NSA_H, NSA_DH ** -0.5), q_ref, nblk)
    ex = ex_ref[...]

    def mult(u, j):
        return _dot_tn(cnt_ref[j][:, u * TS:(u + 1) * TS], ex)

    def step(c, near_for_a):
        kts = [kt_ref[2 * c], kt_ref[2 * c + 1]]
        v = _rows(v_ref, c * QT, QT)
        for u in range(2):
            w0, w1 = mult(u, 2 * c), mult(u, 2 * c + 1)
            ok0, ok1 = w0 > 0.0, w1 > 0.0
            for h in range(nblk):
                q = qs_ref[u * nblk + h]
                s0, s1 = _dot(q, kts[0]), _dot(q, kts[1])
                if near_for_a and u == 0:
                    s1 = s1 + bt_ref[h, 1]
                _flash_rows([jnp.where(ok0, s0, NEG), jnp.where(ok1, s1, NEG)], v, m_ref, acc_ref,
                            u * nblk + h, [w0, w1])

    def far(c, carry):
        step(c, False)
        return carry

    lax.fori_loop(0, jnp.maximum(qi - 1, 0), far, 0)

    @pl.when(qi >= 1)
    def _():
        step(qi - 1, True)

    row, colk = _causal_tile()
    causal = colk <= row
    kt_a, kt_b = kt_ref[2 * qi], kt_ref[2 * qi + 1]
    v_ab = _rows(v_ref, qi * QT, QT)
    w = mult(0, 2 * qi)
    ok = jnp.logical_and(causal, w > 0.0)
    for h in range(nblk):
        s = jnp.where(ok, _dot(qs_ref[h], kt_a) + bt_ref[h, 0], NEG)
        _flash_rows([s], v_ab[:TS], m_ref, acc_ref, h, [w])
    w0, w1 = mult(1, 2 * qi), mult(1, 2 * qi + 1)
    ok0, ok1 = w0 > 0.0, jnp.logical_and(causal, w1 > 0.0)
    for h in range(nblk):
        q = qs_ref[nblk + h]
        s0 = jnp.where(ok0, _dot(q, kt_a) + bt_ref[h, 1], NEG)
        s1 = jnp.where(ok1, _dot(q, kt_b) + bt_ref[h, 0], NEG)
        _flash_rows([s0, s1], v_ab, m_ref, acc_ref, nblk + h, [w0, w1])

    g = _sigmoid(misc_ref[:, 0:3 * NSA_H])
    oc = oc_ref[...]
    for u in range(2):
        t_abs = 2 * qi + u
        j2, j1 = jnp.maximum(t_abs - 2, 0), jnp.maximum(t_abs - 1, 0)
        mask2 = jnp.logical_and(colk >= row, t_abs >= 2)
        mask1 = jnp.logical_and(colk >= 0, t_abs >= 1)
        kt2, kt1, kt0 = wkt_ref[j2], wkt_ref[j1], wkt_ref[t_abs]
        v2, v1, v0 = _rows(wv_ref, j2 * TS, TS), _rows(wv_ref, j1 * TS, TS), _rows(wv_ref, t_abs * TS, TS)
        rs = slice(u * TS, (u + 1) * TS)
        pieces = []
        for h in range(nblk):
            q = qs_ref[u * nblk + h]
            s2 = jnp.where(mask2, _dot(q, kt2), NEG)
            s1 = jnp.where(mask1, _dot(q, kt1) + bt_ref[h, 1], NEG)
            s0 = jnp.where(causal, _dot(q, kt0) + bt_ref[h, 0], NEG)
            mw = _rowmax([s2, s1, s0])
            p2, p1, p0 = jnp.exp(s2 - mw), jnp.exp(s1 - mw), jnp.exp(s0 - mw)
            lw = jnp.sum(p2 + p1 + p0, axis=-1, keepdims=True)
            ow = (_dot(p2.astype(BF16), v2) + _dot(p1.astype(BF16), v1) + _dot(p0.astype(BF16), v0)) / lw
            acc = acc_ref[u * nblk + h]
            o_s = acc[:, NSA_DH:2 * NSA_DH] / acc[:, 2 * NSA_DH:3 * NSA_DH]
            pieces.append(oc[rs, h * NSA_DH:(h + 1) * NSA_DH]
                          + g[rs, NSA_H + h:NSA_H + h + 1] * o_s
                          + g[rs, 2 * NSA_H + h:2 * NSA_H + h + 1] * ow[:, NSA_DH:])
        o_ref[rs, :] = jnp.concatenate(pieces, axis=1)


def _nsa_prompt(us, oc, cnt8, kt, v_ext, wkt, wv, bt, ex8, name):
    t = us.shape[0]
    nblk = 2 * NSA_H
    col = lambda off, w: pl.BlockSpec((QT, w), lambda i: (i, off // w))
    return pl.pallas_call(
        _nsa_prompt_kernel,
        grid=(t // QT,),
        in_specs=[col(_US_NSA_Q, 512), col(_US_MISC, 128),
                  pl.BlockSpec((QT, 512), lambda i: (i, 0)),
                  pl.BlockSpec((cnt8.shape[0], 8, QT), lambda i: (0, 0, i)),
                  _resident(kt.shape), _resident(v_ext.shape), _resident(wkt.shape), _resident(wv.shape),
                  _resident((NSA_H, 2, TS, TS)), _resident(ex8.shape)],
        out_specs=pl.BlockSpec((QT, 512), lambda i: (i, 0)),
        out_shape=jax.ShapeDtypeStruct((t, 512), F32),
        scratch_shapes=[pltpu.VMEM((nblk, TS, 128), BF16), pltpu.VMEM((nblk, TS, 128), F32),
                        pltpu.VMEM((nblk, TS, 256), F32)],
        compiler_params=_cparams(("parallel",)),
        name=name,
    )(us, us, oc, cnt8, kt, v_ext, wkt, wv, bt, ex8)


def _key_tiles(x):
    t, d = x.shape
    return x.reshape(t // TS, TS, d).transpose(0, 2, 1)


def _with_ones(x, n):
    return jnp.concatenate([x, jnp.ones((x.shape[0], n), x.dtype)], axis=1)


def _mem_kernel(q_ref, mkv_ref, o_ref):
    q = q_ref[...]
    outs = []
    for h in range(MEM_H):
        qh = q[:, h * MEM_DH:(h + 1) * MEM_DH].astype(BF16)
        kvh = mkv_ref[0, :, h * 2 * MEM_DH:(h + 1) * 2 * MEM_DH].astype(BF16)
        k, v = kvh[:, :MEM_DH], kvh[:, MEM_DH:]
        s = _dot_nt(qh, k) * (MEM_DH ** -0.5)
        m = jnp.max(s, axis=-1, keepdims=True)
        e = jnp.exp(s - m)
        p = e / jnp.sum(e, axis=-1, keepdims=True)
        outs.append(_dot(p.astype(BF16), v))
    o_ref[...] = jnp.concatenate(outs, axis=1)


def _mem_attend(us, mkv, rows, per_seq, name):
    m = us.shape[0]
    if per_seq:
        mkv_spec = pl.BlockSpec((1,) + mkv.shape[1:], lambda i: (i, 0, 0))
    else:
        mkv_spec = _resident(mkv.shape)
    return pl.pallas_call(
        _mem_kernel,
        grid=(m // rows,),
        in_specs=[pl.BlockSpec((rows, 512), lambda i: (i, _US_MEM_Q // 512)), mkv_spec],
        out_specs=pl.BlockSpec((rows, 512), lambda i: (i, 0)),
        out_shape=jax.ShapeDtypeStruct((m, 512), F32),
        compiler_params=_cparams(("parallel",)),
        name=name,
    )(us, mkv)


def _merge_kernel(x_ref, o0_ref, o1_ref, o2_ref, o3_ref, z_ref, mg_ref, wb_ref, wo_ref, g_ref, y_ref):
    h = None
    for n, o_ref in enumerate((o0_ref, o1_ref, o2_ref, o3_ref)):
        z = z_ref[:, n * BRANCH_W:(n + 1) * BRANCH_W].astype(F32)
        gated = o_ref[...] * (z * _sigmoid(z))
        br = _dot(gated.astype(BF16), wb_ref[n])
        term = _sigmoid(mg_ref[:, n * D_MODEL:(n + 1) * D_MODEL].astype(F32)) * br
        h = term if h is None else h + term
    y = _dot(h.astype(BF16), wo_ref[...])
    y_ref[...] = x_ref[...] + _rms_rows(y, g_ref[...])


def _merge(x, outs, ub, lw, tm, name):
    m = x.shape[0]
    row = lambda w: pl.BlockSpec((tm, w), lambda i: (i, 0))
    return pl.pallas_call(
        _merge_kernel,
        grid=(m // tm,),
        in_specs=[row(D_MODEL), row(512), row(512), row(512), row(512),
                  pl.BlockSpec((tm, N_BRANCH * BRANCH_W), lambda i: (i, N_BRANCH * D_MODEL // (N_BRANCH * BRANCH_W))),
                  pl.BlockSpec((tm, N_BRANCH * D_MODEL), lambda i: (i, 0)),
                  _resident((N_BRANCH, BRANCH_W, D_MODEL)), _resident((D_MODEL, D_MODEL)),
                  _resident((1, D_MODEL))],
        out_specs=row(D_MODEL),
        out_shape=jax.ShapeDtypeStruct((m, D_MODEL), F32),
        compiler_params=_cparams(("parallel",)),
        name=name,
    )(x, *outs, ub, ub, lw['w_branch'], lw['w_out'], lw['post_g'])


def _page_specs(rows, width, n_pages, pps):
    def spec(j):
        return pl.BlockSpec((1, rows, width), lambda b, c, pt: (pt[b * n_pages + c * pps + j], 0, 0))
    return [spec(j) for j in range(pps)]


def _gather_pages(page_refs):
    return jnp.concatenate([r[0] for r in page_refs], axis=0)


def _pool_paged_kernel(pt_ref, *refs):
    pps = len(refs) - 3
    new_ref, o_ref, x_ref = refs[pps], refs[pps + 1], refs[pps + 2]
    means = []
    for r in refs[:pps]:
        means.append(jnp.sum(r[0].reshape(PAGE // NSA_BLOCK, NSA_BLOCK, 128), axis=1) * (1.0 / NSA_BLOCK))
    o_ref[0] = jnp.concatenate(means, axis=0)
    part = jnp.sum(new_ref[...], axis=0, keepdims=True) * (1.0 / NSA_BLOCK)
    rowi = lax.broadcasted_iota(jnp.int32, (128, 128), 0)
    x_ref[0] = jnp.where(rowi == 0, jnp.broadcast_to(part, (128, 128)), 0.0)


def _pool_paged(pt, cache, us, n_seq, n_pages, pps, name):
    nbp = n_pages * (PAGE // NSA_BLOCK)
    per = pps * (PAGE // NSA_BLOCK)
    grid_spec = pltpu.PrefetchScalarGridSpec(
        num_scalar_prefetch=1,
        grid=(n_seq, n_pages // pps),
        in_specs=_page_specs(PAGE, 128, n_pages, pps)
        + [pl.BlockSpec((SAMPLE_ROWS, 128), lambda b, c, pt: (b, _US_CMP // 128))],
        out_specs=[pl.BlockSpec((1, per, 128), lambda b, c, pt: (b, c, 0)),
                   pl.BlockSpec((1, 128, 128), lambda b, c, pt: (b, 0, 0))],
    )
    return pl.pallas_call(
        _pool_paged_kernel,
        grid_spec=grid_spec,
        out_shape=[jax.ShapeDtypeStruct((n_seq, nbp, 128), F32),
                   jax.ShapeDtypeStruct((n_seq, 128, 128), F32)],
        compiler_params=_cparams(("parallel", "arbitrary")),
        name=name,
    )(pt, *([cache] * pps), us)


def _new_row_mask(n_new):
    row = lax.broadcasted_iota(jnp.int32, (SAMPLE_ROWS, SAMPLE_ROWS), 0)
    colk = lax.broadcasted_iota(jnp.int32, (SAMPLE_ROWS, SAMPLE_ROWS), 1)
    return jnp.logical_and(colk <= row, colk < n_new)


def _nsa_paged_kernel(pt_ref, *refs, n_new, cur_blk):
    pps = len(refs) - 16
    pages = refs[:pps]
    (q_ref, misc_ref, oc_ref, cnt_ref, cntx_ref, new_slc_ref, new_win_ref, cwin_ref,
     bl_ref, bn_ref, bw_ref, ex_ref, o_ref, ms, ls, accs) = refs[pps:]
    c = pl.program_id(1)
    nc = pl.num_programs(1)
    r = SAMPLE_ROWS
    hr = NSA_H * r
    qp = _pad_heads_128(q_ref[...], NSA_H, NSA_DH ** -0.5)

    @pl.when(c == 0)
    def _():
        mult = cntx_ref[:, cur_blk:cur_blk + 1].astype(F32)
        kv = new_slc_ref[...].astype(BF16)
        ok = jnp.logical_and(_new_row_mask(n_new), mult > 0.0)
        s = _heads3(_dot_nt(qp, kv), NSA_H) + bn_ref[...]
        s = jnp.where(ok[None], s, NEG).reshape(hr, r)
        w = jnp.broadcast_to(mult[None], (NSA_H, r, r)).reshape(hr, r)
        _flash_init(s, kv, ms, ls, accs, w)

    kv = _gather_pages(pages).astype(BF16)
    w = _dot(cnt_ref[0, 0], ex_ref[...])
    wfull = jnp.broadcast_to(w[None], (NSA_H,) + w.shape).reshape(hr, w.shape[1])
    s3 = _heads3(_dot_nt(qp, kv), NSA_H)

    @pl.when(c < nc - 1)
    def _():
        s = jnp.where((w > 0.0)[None], s3, NEG).reshape(hr, w.shape[1])
        _flash_step(s, kv, ms, ls, accs, wfull)

    @pl.when(c == nc - 1)
    def _():
        s = jnp.where((w > 0.0)[None], s3 + bl_ref[...], NEG).reshape(hr, w.shape[1])
        _flash_step(s, kv, ms, ls, accs, wfull)
        o_s = accs[...] / ls[...]
        kvw = cwin_ref[0].astype(BF16)
        wb = cwin_ref.shape[1]
        row = lax.broadcasted_iota(jnp.int32, (r, wb), 0)
        colk = lax.broadcasted_iota(jnp.int32, (r, wb), 1)
        sw = _heads3(_dot_nt(qp, kvw), NSA_H) + bw_ref[...]
        sw = jnp.where((colk >= row)[None], sw, NEG).reshape(hr, wb)
        kvn = new_win_ref[...].astype(BF16)
        sn = _heads3(_dot_nt(qp, kvn), NSA_H) + bn_ref[...]
        sn = jnp.where(_new_row_mask(n_new)[None], sn, NEG).reshape(hr, r)
        m = jnp.maximum(jnp.max(sw, axis=-1, keepdims=True), jnp.max(sn, axis=-1, keepdims=True))
        pw = jnp.exp(sw - m)
        pn = jnp.exp(sn - m)
        den = jnp.sum(pw, axis=-1, keepdims=True) + jnp.sum(pn, axis=-1, keepdims=True)
        o_w = (_dot(pw.astype(BF16), kvw) + _dot(pn.astype(BF16), kvn)) / den
        g = _sigmoid(misc_ref[:, 0:3 * NSA_H])
        oc = oc_ref[...]
        pieces = []
        for h in range(NSA_H):
            sl = slice(h * r, (h + 1) * r)
            pieces.append(oc[:, h * NSA_DH:(h + 1) * NSA_DH]
                          + g[:, NSA_H + h:NSA_H + h + 1] * o_s[sl, NSA_DH:]
                          + g[:, 2 * NSA_H + h:2 * NSA_H + h + 1] * o_w[sl, NSA_DH:])
        o_ref[...] = jnp.concatenate(pieces, axis=1)


def _nsa_paged(pt, cache, us, oc, cnt4, cntx, cwin, seq0, bl, bn, bw, ex, n_seq, n_pages, pps, n_new, name):
    r = SAMPLE_ROWS
    hr = NSA_H * r
    ck = pps * PAGE
    col = lambda off, w: pl.BlockSpec((r, w), lambda b, c, pt: (b, off // w))
    res = lambda shape: pl.BlockSpec(shape, lambda b, c, pt: (0,) * len(shape))
    nbk = cntx.shape[1]
    grid_spec = pltpu.PrefetchScalarGridSpec(
        num_scalar_prefetch=1,
        grid=(n_seq, n_pages // pps),
        in_specs=_page_specs(PAGE, 128, n_pages, pps) + [
            col(_US_NSA_Q, 512), col(_US_MISC, 128),
            pl.BlockSpec((r, 512), lambda b, c, pt: (b, 0)),
            pl.BlockSpec((1, 1, r, cnt4.shape[3]), lambda b, c, pt: (b, c, 0, 0)),
            pl.BlockSpec((r, nbk), lambda b, c, pt: (b, 0)),
            col(_US_SLC, 128), col(_US_WIN, 128),
            pl.BlockSpec((1,) + cwin.shape[1:], lambda b, c, pt: (b + seq0, 0, 0)),
            res((NSA_H, r, ck)), res((NSA_H, r, r)), res((NSA_H, r, cwin.shape[1])), res(ex.shape)],
        out_specs=pl.BlockSpec((r, 512), lambda b, c, pt: (b, 0)),
        scratch_shapes=[pltpu.VMEM((hr, 1), F32), pltpu.VMEM((hr, 1), F32), pltpu.VMEM((hr, 128), F32)],
    )
    cur_blk = n_pages * (PAGE // NSA_BLOCK)
    return pl.pallas_call(
        functools.partial(_nsa_paged_kernel, n_new=n_new, cur_blk=cur_blk),
        grid_spec=grid_spec,
        out_shape=jax.ShapeDtypeStruct((n_seq * r, 512), F32),
        compiler_params=_cparams(("parallel", "arbitrary")),
        name=name,
    )(pt, *([cache] * pps), us, us, oc, cnt4, cntx, us, us, cwin, bl, bn, bw, ex)


def _mla_finish(acc, l, wuv_ref, o_ref, rows):
    o_lat = acc / l
    pieces = [_dot(o_lat[h * rows:(h + 1) * rows].astype(BF16), wuv_ref[h]) for h in range(MLA_H)]
    o_ref[...] = jnp.concatenate(pieces, axis=1)


def _mla_paged_kernel(pt_ref, *refs, n_new):
    pps = len(refs) - 7
    pages = refs[:pps]
    q_ref, new_ref, wuv_ref, o_ref, m_ref, l_ref, acc_ref = refs[pps:]
    c = pl.program_id(1)
    r = SAMPLE_ROWS
    q = q_ref[...].reshape(MLA_H * r, MLA_LAT)

    @pl.when(c == 0)
    def _():
        k = new_ref[...].astype(BF16)
        s = _heads3(_dot_nt(q, k), MLA_H)
        s = jnp.where(_new_row_mask(n_new)[None], s, NEG).reshape(MLA_H * r, r)
        _flash_init(s, k[:, :MLA_KV_RANK], m_ref, l_ref, acc_ref)

    kt = jnp.concatenate([p[0] for p in pages], axis=1).astype(BF16)
    _flash_step(_dot(q, kt), kt[:MLA_KV_RANK], m_ref, l_ref, acc_ref, v_feature_major=True)

    @pl.when(c == pl.num_programs(1) - 1)
    def _():
        _mla_finish(acc_ref[...], l_ref[...], wuv_ref, o_ref, r)


def _mla_paged(pt, cache, q_mla, lat_new, wuv, n_seq, n_pages, pps, n_new, name):
    r = SAMPLE_ROWS
    hr = MLA_H * r
    grid_spec = pltpu.PrefetchScalarGridSpec(
        num_scalar_prefetch=1,
        grid=(n_seq, n_pages // pps),
        in_specs=_page_specs(MLA_LAT, PAGE, n_pages, pps) + [
            pl.BlockSpec((MLA_H, r, MLA_LAT), lambda b, c, pt: (0, b, 0)),
            pl.BlockSpec((r, MLA_LAT), lambda b, c, pt: (b, 0)),
            pl.BlockSpec((MLA_H, MLA_KV_RANK, MLA_DV), lambda b, c, pt: (0, 0, 0))],
        out_specs=pl.BlockSpec((r, 512), lambda b, c, pt: (b, 0)),
        scratch_shapes=[pltpu.VMEM((hr, 1), F32), pltpu.VMEM((hr, 1), F32),
                        pltpu.VMEM((hr, MLA_KV_RANK), F32)],
    )
    return pl.pallas_call(
        functools.partial(_mla_paged_kernel, n_new=n_new),
        grid_spec=grid_spec,
        out_shape=jax.ShapeDtypeStruct((n_seq * r, 512), F32),
        compiler_params=_cparams(("parallel", "arbitrary")),
        name=name,
    )(pt, *([cache] * pps), q_mla, lat_new, wuv)


def _diff_paged_kernel(pt_ref, *refs, n_new, lam_init):
    pps = len(refs) - 10
    pages = refs[:pps]
    q_ref, new_ref, bl_ref, bn_ref, lp_ref, g_ref, o_ref, m_ref, l_ref, acc_ref = refs[pps:]
    c = pl.program_id(1)
    nc = pl.num_programs(1)
    r = SAMPLE_ROWS
    hr = 2 * DIFF_H * r
    q = _diff_queries(q_ref[...])

    def scores(k, bias, mask):
        s = _heads3(_dot_nt(q, k), 2 * DIFF_H)
        if bias is not None:
            s = s + jnp.concatenate([bias, bias], axis=0)
        if mask is not None:
            s = jnp.where(mask[None], s, NEG)
        return s.reshape(hr, k.shape[0])

    @pl.when(c == 0)
    def _():
        kv = new_ref[...].astype(BF16)
        _flash_init(scores(kv[:, :2 * DIFF_DH], bn_ref[...], _new_row_mask(n_new)), kv[:, 2 * DIFF_DH:],
                    m_ref, l_ref, acc_ref)

    k = jnp.concatenate([p[0, pl.ds(0, PAGE, stride=2), :] for p in pages], axis=0).astype(BF16)
    v = jnp.concatenate([p[0, pl.ds(1, PAGE, stride=2), :] for p in pages], axis=0).astype(BF16)

    @pl.when(c < nc - 1)
    def _():
        _flash_step(scores(k, None, None), v, m_ref, l_ref, acc_ref)

    @pl.when(c == nc - 1)
    def _():
        _flash_step(scores(k, bl_ref[...], None), v, m_ref, l_ref, acc_ref)
        _diff_finish(acc_ref[...], l_ref[...], lp_ref, g_ref, o_ref, r, lam_init)


def _diff_paged(pt, cache, us, bl, bn, lp, g, lam_init, n_seq, n_pages, pps, n_new, name):
    r = SAMPLE_ROWS
    hr = 2 * DIFF_H * r
    ck = pps * PAGE
    res = lambda shape: pl.BlockSpec(shape, lambda b, c, pt: (0,) * len(shape))
    grid_spec = pltpu.PrefetchScalarGridSpec(
        num_scalar_prefetch=1,
        grid=(n_seq, n_pages // pps),
        in_specs=_page_specs(2 * PAGE, 2 * DIFF_DH, n_pages, pps) + [
            pl.BlockSpec((r, 512), lambda b, c, pt: (b, _US_DIFF_Q // 512)),
            pl.BlockSpec((r, 4 * DIFF_DH), lambda b, c, pt: (b, _US_DIFF_KV // (4 * DIFF_DH))),
            res((DIFF_H, r, ck)), res((DIFF_H, r, r)), res((4, DIFF_DH)), res((1, 2 * DIFF_DH))],
        out_specs=pl.BlockSpec((r, 512), lambda b, c, pt: (b, 0)),
        scratch_shapes=[pltpu.VMEM((hr, 1), F32), pltpu.VMEM((hr, 1), F32),
                        pltpu.VMEM((hr, 2 * DIFF_DH), F32)],
    )
    return pl.pallas_call(
        functools.partial(_diff_paged_kernel, n_new=n_new, lam_init=lam_init),
        grid_spec=grid_spec,
        out_shape=jax.ShapeDtypeStruct((n_seq * r, 512), F32),
        compiler_params=_cparams(("parallel", "arbitrary")),
        name=name,
    )(pt, *([cache] * pps), us, us, bl, bn, lp, g)


def _layer_weights(l, w_in, pre_norm_g, post_norm_g, mla_q_norm_g, w_mla_uq, mla_kv_norm_g, w_mla_uk,
                   w_mla_uv, lam_params, diff_subln_g, mem_norm_g, w_mem_kv, w_branch, w_out):
    w = w_in[l]
    widths = [NSA_H * NSA_DH, 128, 128, 128, 3 * NSA_H, MLA_Q_RANK, MLA_KV_RANK, MLA_ROPE,
              DIFF_H * 2 * DIFF_DH, 4 * DIFF_DH, MEM_H * MEM_DH, N_BRANCH * BRANCH_W, N_BRANCH * D_MODEL]
    offs = np.concatenate([[0], np.cumsum(widths)])
    (nsa_q, cmp_, slc, win, gate, cq, ckv, kr, diff_q, diff_kv, mem_q, z, mg) = [
        w[:, offs[i]:offs[i + 1]] for i in range(len(widths))]
    half = MLA_ROPE // 2
    gate_t = gate.reshape(D_MODEL, NSA_H, 3).transpose(0, 2, 1).reshape(D_MODEL, 3 * NSA_H)
    kr_sw = jnp.concatenate([kr[:, half:], kr[:, :half]], axis=1)
    misc = jnp.concatenate([gate_t, kr, kr_sw, jnp.zeros((D_MODEL, 128 - 3 * NSA_H - 2 * MLA_ROPE), F32)], axis=1)
    w_small = jnp.concatenate([nsa_q, diff_q, mem_q, cq, diff_kv, cmp_, slc, win, ckv, misc], axis=1)
    w_big = jnp.concatenate([mg, z], axis=1)
    uq = w_mla_uq[l].reshape(MLA_Q_RANK, MLA_H, MLA_NOPE + MLA_ROPE)
    uq_r = uq[:, :, MLA_NOPE:]
    uq_rs = jnp.concatenate([uq_r[:, :, half:], uq_r[:, :, :half]], axis=2)
    return {
        'pre_g': pre_norm_g[l], 'post_g': post_norm_g[l].reshape(1, D_MODEL),
        'w_small': w_small.astype(BF16), 'w_big': w_big.astype(BF16),
        'gq': mla_q_norm_g[l].reshape(1, MLA_Q_RANK), 'gkv': mla_kv_norm_g[l].reshape(1, MLA_KV_RANK),
        'w_uq_n': uq[:, :, :MLA_NOPE].reshape(MLA_Q_RANK, MLA_H * MLA_NOPE).astype(BF16),
        'w_uq_r': uq_r.reshape(MLA_Q_RANK, MLA_H * MLA_ROPE).astype(BF16),
        'w_uq_rs': uq_rs.reshape(MLA_Q_RANK, MLA_H * MLA_ROPE).astype(BF16),
        'w_ukT': w_mla_uk[l].reshape(MLA_KV_RANK, MLA_H, MLA_NOPE).transpose(1, 2, 0).astype(BF16),
        'w_uv': w_mla_uv[l].reshape(MLA_KV_RANK, MLA_H, MLA_DV).transpose(1, 0, 2).astype(BF16),
        'lam_params': lam_params[l], 'subln_g': diff_subln_g[l].reshape(1, 2 * DIFF_DH),
        'mem_g': mem_norm_g[l], 'w_mem_kv': w_mem_kv[l].astype(BF16),
        'w_branch': w_branch[l].astype(BF16), 'w_out': w_out[l].astype(BF16),
    }


def _rope_tables(pos):
    half = MLA_ROPE // 2
    inv = ROPE_THETA ** (-jnp.arange(half, dtype=F32) / half)
    ang = pos.astype(F32)[:, None] * inv[None, :]
    cos, sin = jnp.cos(ang), jnp.sin(ang)
    return jnp.concatenate([cos, cos], axis=1), jnp.concatenate([-sin, sin], axis=1)


def _in_proj(x, lw, tm, tag):
    ub = _rms_matmul(x, lw['pre_g'], lw['w_big'], BF16, tm, 2048, tag + '_proj_big')
    us = _rms_matmul(x, lw['pre_g'], lw['w_small'], F32, tm, _US_W, tag + '_proj_small')
    return ub, us


def kernel(x_prompt, x_sample, cache_nsa_cmp_kv, cache_nsa_slc_kv, cache_nsa_win_kv, cache_mla_latent, cache_diff_kv, cache_mem_kv, page_table, mem_prompt, rel_bias, pre_norm_g, post_norm_g, w_in, mla_q_norm_g, w_mla_uq, mla_kv_norm_g, w_mla_uk, w_mla_uv, diff_lambda_q1, diff_lambda_k1, diff_lambda_q2, diff_lambda_k2, diff_subln_g, mem_norm_g, w_mem_kv, w_branch, w_out):
    depth = w_in.shape[0]
    bp, t, _ = x_prompt.shape
    assert bp == 1, "prompt group is a single sequence"
    ns, n_new, _ = x_sample.shape
    n_pages = page_table.shape[1]
    past = n_pages * PAGE
    n_pool = cache_nsa_cmp_kv.shape[1]
    wbuf = cache_nsa_win_kv.shape[2]
    assert wbuf == NSA_WINDOW and n_new <= SAMPLE_ROWS and past % NSA_BLOCK == 0
    assert t % QT == 0 and NSA_WINDOW == 2 * TS
    pps = min(PAGES_PER_STEP, n_pages)
    assert n_pages % pps == 0
    ck = pps * PAGE
    r = SAMPLE_ROWS
    nb_p = t // NSA_BLOCK
    nb_s = past // NSA_BLOCK
    nbk_s = nb_s + 128
    bpt = TS // NSA_BLOCK
    spc = max(d for d in range(1, SEQS_PER_CMP_STEP + 1) if ns % d == 0)

    lam_params = jnp.stack([diff_lambda_q1, diff_lambda_k1, diff_lambda_q2, diff_lambda_k2], axis=1).astype(F32)

    pos_p = jnp.arange(t, dtype=jnp.int32)
    tok = jnp.arange(r, dtype=jnp.int32)
    pos_s = jnp.tile(past + tok, ns)
    cos_p, sin_p = _rope_tables(pos_p)
    cos_s, sin_s = _rope_tables(pos_s)
    ar = lambda n: jnp.arange(n, dtype=jnp.int32)
    d_tile = jnp.stack([ar(TS)[:, None] - ar(TS)[None, :], TS + ar(TS)[:, None] - ar(TS)[None, :]]).reshape(2 * TS, TS)
    d_cmp_p = pos_p[:, None] - (ar(nb_p)[None, :] * NSA_BLOCK + NSA_BLOCK - 1)
    d_cmp_s = (past + tok)[:, None] - (ar(nbk_s)[None, :] * NSA_BLOCK + NSA_BLOCK - 1)
    d_last = ck + tok[:, None] - ar(ck)[None, :]
    d_new = tok[:, None] - tok[None, :]
    d_win = wbuf + tok[:, None] - ar(wbuf)[None, :]
    bt = _bias_expand(rel_bias, d_tile, 0, BIAS_H, 'bias_tiles').reshape(BIAS_H, 2, TS, TS)
    bias_cmp_p = _bias_expand(rel_bias, d_cmp_p, 0, NSA_H, 'bias_cmp_prompt')
    bias_cmp_s = _bias_expand(rel_bias, d_cmp_s, 0, NSA_H, 'bias_cmp_sample')
    bias_last = _bias_expand(rel_bias, d_last, 0, BIAS_H, 'bias_last_chunk')
    bias_new = _bias_expand(rel_bias, d_new, 0, BIAS_H, 'bias_new_rows')
    bias_win = _bias_expand(rel_bias, d_win, 0, NSA_H, 'bias_window')
    ex8 = np.zeros((8, TS), np.float32)
    for cidx in range(bpt):
        ex8[cidx, cidx * NSA_BLOCK:(cidx + 1) * NSA_BLOCK] = 1.0
    ex8 = jnp.asarray(ex8, BF16)
    ex = jnp.asarray(np.repeat(np.eye(ck // NSA_BLOCK, dtype=np.float32), NSA_BLOCK, axis=1), BF16)

    pt_flat = page_table.reshape(-1).astype(jnp.int32)
    c_cmp = cache_nsa_cmp_kv.reshape(depth * n_pool, PAGE, 128)
    c_slc = cache_nsa_slc_kv.reshape(depth * n_pool, PAGE, 128)
    c_lat = jnp.swapaxes(cache_mla_latent, 2, 3).reshape(depth * n_pool, MLA_LAT, PAGE)
    c_dkv = cache_diff_kv.reshape(depth * n_pool, 2 * PAGE, 2 * DIFF_DH)
    c_win = cache_nsa_win_kv.reshape(depth * ns, wbuf, 128)

    xp = x_prompt.reshape(t, D_MODEL)
    xs = jnp.pad(x_sample, ((0, 0), (0, r - n_new), (0, 0))).reshape(ns * r, D_MODEL)
    p_st = [[] for _ in range(6)]
    s_st = [[] for _ in range(5)]
    for l in range(depth):
        lam_init = 0.8 - 0.6 * math.exp(-0.3 * l)
        lw = _layer_weights(l, w_in, pre_norm_g, post_norm_g, mla_q_norm_g, w_mla_uq, mla_kv_norm_g, w_mla_uk,
                            w_mla_uv, lam_params, diff_subln_g, mem_norm_g, w_mem_kv, w_branch, w_out)
        tag = 'l%d' % l
        ub, us = _in_proj(xp, lw, min(512, t), tag + '_p')
        q_mla, lat = _mla_prep(us, cos_p, sin_p, lw, min(512, t), True, tag + '_p_mla_prep')
        kvc = _pool_prompt(us, tag + '_p_pool')
        oc, cnt = _nsa_cmp(us, kvc, bias_cmp_p, pos_p.reshape(t, 1), nb_p, QT, 0, tag + '_p_nsa_cmp')
        cmp_new = us[:, _US_CMP:_US_CMP + 128]
        slc_new = us[:, _US_SLC:_US_SLC + 128]
        win_new = us[:, _US_WIN:_US_WIN + 128]
        dkv_new = us[:, _US_DIFF_KV:_US_DIFF_KV + 4 * DIFF_DH]
        cnt8 = jnp.pad(cnt.reshape(t, nb_p // bpt, bpt).transpose(1, 2, 0), ((0, 0), (0, 8 - bpt), (0, 0)))
        slc_bf, win_bf = slc_new.astype(BF16), win_new.astype(BF16)
        o_nsa = _nsa_prompt(us, oc, cnt8, _key_tiles(slc_bf), _with_ones(slc_bf, 128), _key_tiles(win_bf), win_bf,
                            bt[:NSA_H], ex8, tag + '_p_nsa_attn')
        lat_bf = lat.astype(BF16)
        mla_k = jnp.concatenate([lat_bf[:, MLA_KV_RANK:], jnp.zeros((t, MLA_QW - MLA_LAT), BF16),
                                 lat_bf[:, :MLA_KV_RANK]], axis=1)
        o_mla = _mla_prompt(q_mla, _key_tiles(mla_k), _with_ones(lat_bf[:, :MLA_KV_RANK], MLA_KV_RANK), lw['w_uv'],
                            tag + '_p_mla_attn')
        dkv_bf = dkv_new.astype(BF16)
        o_diff = _diff_prompt(us, _key_tiles(dkv_bf[:, :2 * DIFF_DH]), _with_ones(dkv_bf[:, 2 * DIFF_DH:], 2 * DIFF_DH),
                              bt[NSA_H:], lw['lam_params'], lw['subln_g'], lam_init, tag + '_p_diff_attn')
        mkv = _rms_matmul(mem_prompt.reshape(N_MEM, D_MODEL), lw['mem_g'], lw['w_mem_kv'], F32,
                          N_MEM, MEM_H * 2 * MEM_DH, tag + '_p_mem_kv')
        o_mem = _mem_attend(us, mkv.reshape(1, N_MEM, -1), TS, False, tag + '_p_mem_attn')
        xp = _merge(xp, [o_nsa, o_mla, o_diff, o_mem], ub, lw, TS, tag + '_p_merge')
        win_keep = min(NSA_WINDOW, t)
        for i, a in enumerate((cmp_new.reshape(1, t, 1, 128), slc_new.reshape(1, t, 1, 128),
                               win_new[t - win_keep:].reshape(1, win_keep, 1, 128), lat.reshape(1, t, MLA_LAT),
                               dkv_new.reshape(1, t, 1, 4 * DIFF_DH), mkv.reshape(1, N_MEM, MEM_H, 2 * MEM_DH))):
            p_st[i].append(a)
        pt_l = pt_flat + l * n_pool
        ub, us = _in_proj(xs, lw, min(512, ns * r), tag + '_s')
        q_mla, lat = _mla_prep(us, cos_s, sin_s, lw, min(512, ns * r), False, tag + '_s_mla_prep')
        kvc_past, kvc_part = _pool_paged(pt_l, c_cmp, us, ns, n_pages, pps, tag + '_s_pool')
        kvc = jnp.concatenate([kvc_past, kvc_part], axis=1)
        oc, cnt = _nsa_cmp(us, kvc, bias_cmp_s, pos_s.reshape(ns * r, 1), nb_s + 1, spc * r, spc, tag + '_s_nsa_cmp')
        cnt4 = cnt[:, :nb_s].reshape(ns, r, n_pages // pps, ck // NSA_BLOCK).transpose(0, 2, 1, 3)
        o_nsa = _nsa_paged(pt_l, c_slc, us, oc, cnt4, cnt, c_win, l * ns, bias_last[:NSA_H], bias_new[:NSA_H],
                           bias_win, ex, ns, n_pages, pps, n_new, tag + '_s_nsa_attn')
        o_mla = _mla_paged(pt_l, c_lat, q_mla, lat, lw['w_uv'], ns, n_pages, pps, n_new, tag + '_s_mla_attn')
        o_diff = _diff_paged(pt_l, c_dkv, us, bias_last[NSA_H:], bias_new[NSA_H:], lw['lam_params'], lw['subln_g'],
                             lam_init, ns, n_pages, pps, n_new, tag + '_s_diff_attn')
        c_mem = cache_mem_kv[l].reshape(ns, N_MEM, MEM_H * 2 * MEM_DH)
        o_mem = _mem_attend(us, c_mem, r, True, tag + '_s_mem_attn')
        xs = _merge(xs, [o_nsa, o_mla, o_diff, o_mem], ub, lw, min(256, ns * r), tag + '_s_merge')
        us3 = us.reshape(ns, r, _US_W)[:, :n_new]
        new_win = us3[:, :, _US_WIN:_US_WIN + 128]
        s_win = jnp.concatenate([c_win[l * ns:(l + 1) * ns, n_new:], new_win], axis=1)
        for i, a in enumerate((us3[:, :, _US_CMP:_US_CMP + 128].reshape(ns, n_new, 1, 128),
                               us3[:, :, _US_SLC:_US_SLC + 128].reshape(ns, n_new, 1, 128),
                               s_win.reshape(ns, wbuf, 1, 128),
                               lat.reshape(ns, r, MLA_LAT)[:, :n_new],
                               us3[:, :, _US_DIFF_KV:_US_DIFF_KV + 4 * DIFF_DH].reshape(ns, n_new, 1, 4 * DIFF_DH))):
            s_st[i].append(a)
    p_nsa_cmp, p_nsa_slc, p_nsa_win, p_mla_latent, p_diff_kv, p_mem_kv = [jnp.stack(a, axis=0) for a in p_st]
    s_nsa_cmp, s_nsa_slc, s_nsa_win, s_mla_latent, s_diff_kv = [jnp.stack(a, axis=0) for a in s_st]
    y_prompt = xp.reshape(1, t, D_MODEL)
    y_sample = xs.reshape(ns, r, D_MODEL)[:, :n_new]
    return (y_prompt, y_sample, p_nsa_cmp, s_nsa_cmp, p_nsa_slc, s_nsa_slc, p_nsa_win, s_nsa_win,
            p_mla_latent, s_mla_latent, p_diff_kv, s_diff_kv, p_mem_kv)
```
